```python
import math
import jax, jax.numpy as jnp
from jax import lax
import numpy as np

D_MODEL = 1024
BATCH = 8
SEQ = 8192
DEPTH = 4
DEC_BATCH = 32
DEC_SEQ = 32
PAST_LEN = 2048

CHUNK = 64
Q_BLOCK = 128
ATT_WIDTH = D_MODEL // 2
CONV_WIDTH = D_MODEL - ATT_WIDTH
N_HEADS = 4
HEAD_V = ATT_WIDTH // N_HEADS
HEAD_QK = HEAD_V // 2
DW_WIDTH = 31
CONV_BUF = DW_WIDTH - 1
IN_COLS = 3 * ATT_WIDTH + 2 * CONV_WIDTH
D_FF = 2816
N_EXPERTS = 8
TOP_K = 2
D_FF_EXPERT = 3584
D_PLE = 256
N_DENSE = (DEPTH + 1) // 2
N_MOE = DEPTH // 2
EPS = 1e-6

kernel_name = "hymba_diffattn_conformer_stream_step"


def rmsnorm(x, g):
    xf = x.astype(jnp.float32)
    y = xf * lax.rsqrt(jnp.mean(xf * xf, axis=-1, keepdims=True) + EPS)
    return (y * g.astype(jnp.float32)).astype(x.dtype)


def layernorm(x, g, b):
    xf = x.astype(jnp.float32)
    mu = jnp.mean(xf, axis=-1, keepdims=True)
    var = jnp.mean(jnp.square(xf - mu), axis=-1, keepdims=True)
    y = (xf - mu) * lax.rsqrt(var + EPS)
    return (y * g.astype(jnp.float32) + b.astype(jnp.float32)).astype(x.dtype)


def diff_attend(q, k, v, lam, mask):
    s = jnp.einsum('bqhcd,bkhcd->bchqk', q, k).astype(jnp.float32) * (HEAD_QK ** -0.5)
    if mask is not None:
        s = jnp.where(mask, s, -jnp.inf)
    p = jax.nn.softmax(s, axis=-1)
    a = p[:, 0] - lam * p[:, 1]
    return jnp.einsum('bhqk,bkhd->bqhd', a.astype(v.dtype), v)


def attend_prompt(q, k, v, lam):
    B, S = q.shape[0], q.shape[1]
    nb = S // Q_BLOCK
    qb = q.reshape(B, nb, Q_BLOCK, N_HEADS, 2, HEAD_QK).transpose(1, 0, 2, 3, 4, 5)
    k_chunk = jnp.arange(S) // CHUNK

    def block(args):
        qi, bidx = args
        q_chunk = (bidx * Q_BLOCK + jnp.arange(Q_BLOCK)) // CHUNK
        mask = k_chunk[None, :] <= q_chunk[:, None]
        return diff_attend(qi, k, v, lam, mask)

    out = lax.map(block, (qb, jnp.arange(nb)))
    return out.transpose(1, 0, 2, 3, 4).reshape(B, S, N_HEADS, HEAD_V)


def dwconv(u_ext, w, b):
    y = lax.conv_general_dilated(u_ext, w[:, None, :], window_strides=(1,), padding='VALID',
                                 dimension_numbers=('NWC', 'WIO', 'NWC'),
                                 feature_group_count=CONV_WIDTH)
    return y + b


def swiglu(x, wg, wu, wd):
    return (jax.nn.silu(x @ wg) * (x @ wu)) @ wd


def moe_ffn(x, router_w, wg, wu, wd):
    B, T, D = x.shape
    xt = x.reshape(B * T, D)
    logits = (xt @ router_w).astype(jnp.float32)
    top_v, top_i = lax.top_k(logits, TOP_K)
    gates = jax.nn.softmax(top_v, axis=-1)
    comb = jnp.sum(jax.nn.one_hot(top_i, N_EXPERTS, dtype=jnp.float32) * gates[..., None], axis=-2)
    comb = comb.astype(x.dtype)
    y = jnp.zeros_like(xt)
    for e in range(N_EXPERTS):
        y = y + comb[:, e:e + 1] * swiglu(xt, wg[e], wu[e], wd[e])
    return y.reshape(B, T, D)


def run_trunk(x, p, cache_k, cache_v, state_conv, norm_mix_g, w_in, lam_q1, lam_k1, lam_q2, lam_k2,
              subln_g, dw_w, dw_b, conv_ln_g, conv_ln_b, w_o, norm_ffn_g, ff_wg, ff_wu, ff_wd,
              router_w, moe_wg, moe_wu, moe_wd, ple_norm_g, ple_gate_w, ple_proj_w, final_norm_g):
    B, T, _ = x.shape
    h = x
    ks, vs, cs = [], [], []
    for i in range(DEPTH):
        hn = rmsnorm(h, norm_mix_g[i])
        proj = hn @ w_in[i]
        q, k, v, g = jnp.split(proj, [ATT_WIDTH, 2 * ATT_WIDTH, 3 * ATT_WIDTH], axis=-1)
        q = q.reshape(B, T, N_HEADS, 2, HEAD_QK)
        k = k.reshape(B, T, N_HEADS, 2 * HEAD_QK)
        v = v.reshape(B, T, N_HEADS, HEAD_V)
        lam_init = 0.8 - 0.6 * math.exp(-0.3 * i)
        lam = (jnp.exp(jnp.sum(lam_q1[i] * lam_k1[i]).astype(jnp.float32))
               - jnp.exp(jnp.sum(lam_q2[i] * lam_k2[i]).astype(jnp.float32)) + lam_init)
        if cache_k is None:
            o = attend_prompt(q, k.reshape(B, T, N_HEADS, 2, HEAD_QK), v, lam)
            hist = jnp.zeros((B, CONV_BUF, CONV_WIDTH), x.dtype)
        else:
            kk = jnp.concatenate([cache_k[i].astype(k.dtype), k], axis=1)
            vv = jnp.concatenate([cache_v[i].astype(v.dtype), v], axis=1)
            o = diff_attend(q, kk.reshape(B, kk.shape[1], N_HEADS, 2, HEAD_QK), vv, lam, None)
            hist = state_conv[i]
        o = (rmsnorm(o, subln_g[i]) * (1.0 - lam_init)).reshape(B, T, ATT_WIDTH)
        ga, gb = jnp.split(g, 2, axis=-1)
        u = ga * jax.nn.sigmoid(gb)
        u_ext = jnp.concatenate([hist.astype(u.dtype), u], axis=1)
        c = dwconv(u_ext, dw_w[i], dw_b[i])
        c = jax.nn.silu(layernorm(c, conv_ln_g[i], conv_ln_b[i]))
        h = h + jnp.concatenate([o, c], axis=-1) @ w_o[i]
        hn = rmsnorm(h, norm_ffn_g[i])
        j = i // 2
        if i % 2 == 0:
            h = h + swiglu(hn, ff_wg[j], ff_wu[j], ff_wd[j])
        else:
            h = h + moe_ffn(hn, router_w[j], moe_wg[j], moe_wu[j], moe_wd[j])
        gate = jax.nn.sigmoid(rmsnorm(h, ple_norm_g[i]) @ ple_gate_w[i])
        h = h + gate * (p[i] @ ple_proj_w[i])
        ks.append(k)
        vs.append(v)
        cs.append(u_ext[:, -CONV_BUF:])
    return rmsnorm(h, final_norm_g), jnp.stack(ks), jnp.stack(vs), jnp.stack(cs)


def setup_inputs(seed: int = 0) -> dict:
    key = jax.random.key(seed)
    ks = jax.random.split(key, 40)
    f32 = jnp.float32

    def nrm(k, shape, scale):
        return jax.random.normal(k, shape, f32) * scale

    def gain(k, shape):
        return 1.0 + 0.05 * jax.random.normal(k, shape, f32)

    D = D_MODEL
    return {
        "x_prompt": nrm(ks[0], (BATCH, SEQ, D), 1.0),
        "x_sample": nrm(ks[1], (DEC_BATCH, DEC_SEQ, D), 1.0),
        "cache_k": nrm(ks[2], (DEPTH, DEC_BATCH, PAST_LEN, N_HEADS, 2 * HEAD_QK), 1.0),
        "cache_v": nrm(ks[3], (DEPTH, DEC_BATCH, PAST_LEN, N_HEADS, HEAD_V), 1.0),
        "state_conv": nrm(ks[4], (DEPTH, DEC_BATCH, CONV_BUF, CONV_WIDTH), 0.5),
        "p_prompt": nrm(ks[5], (DEPTH, BATCH, SEQ, D_PLE), 1.0),
        "p_sample": nrm(ks[6], (DEPTH, DEC_BATCH, DEC_SEQ, D_PLE), 1.0),
        "norm_mix_g": gain(ks[7], (DEPTH, D)),
        "w_in": nrm(ks[8], (DEPTH, D, IN_COLS), D ** -0.5),
        "lam_q1": nrm(ks[9], (DEPTH, HEAD_QK), 0.1),
        "lam_k1": nrm(ks[10], (DEPTH, HEAD_QK), 0.1),
        "lam_q2": nrm(ks[11], (DEPTH, HEAD_QK), 0.1),
        "lam_k2": nrm(ks[12], (DEPTH, HEAD_QK), 0.1),
        "subln_g": gain(ks[13], (DEPTH, HEAD_V)),
        "dw_w": nrm(ks[14], (DEPTH, DW_WIDTH, CONV_WIDTH), DW_WIDTH ** -0.5),
        "dw_b": nrm(ks[15], (DEPTH, CONV_WIDTH), 0.02),
        "conv_ln_g": gain(ks[16], (DEPTH, CONV_WIDTH)),
        "conv_ln_b": nrm(ks[17], (DEPTH, CONV_WIDTH), 0.02),
        "w_o": nrm(ks[18], (DEPTH, D, D), D ** -0.5),
        "norm_ffn_g": gain(ks[19], (DEPTH, D)),
        "ff_wg": nrm(ks[20], (N_DENSE, D, D_FF), D ** -0.5),
        "ff_wu": nrm(ks[21], (N_DENSE, D, D_FF), D ** -0.5),
        "ff_wd": nrm(ks[22], (N_DENSE, D_FF, D), D_FF ** -0.5),
        "router_w": nrm(ks[23], (N_MOE, D, N_EXPERTS), D ** -0.5),
        "moe_wg": nrm(ks[24], (N_MOE, N_EXPERTS, D, D_FF_EXPERT), D ** -0.5),
        "moe_wu": nrm(ks[25], (N_MOE, N_EXPERTS, D, D_FF_EXPERT), D ** -0.5),
        "moe_wd": nrm(ks[26], (N_MOE, N_EXPERTS, D_FF_EXPERT, D), D_FF_EXPERT ** -0.5),
        "ple_norm_g": gain(ks[27], (DEPTH, D)),
        "ple_gate_w": nrm(ks[28], (DEPTH, D, D), D ** -0.5),
        "ple_proj_w": nrm(ks[29], (DEPTH, D_PLE, D), D_PLE ** -0.5),
        "final_norm_g": gain(ks[30], (D,)),
    }


def reference(x_prompt, x_sample, cache_k, cache_v, state_conv, p_prompt, p_sample, norm_mix_g, w_in,
              lam_q1, lam_k1, lam_q2, lam_k2, subln_g, dw_w, dw_b, conv_ln_g, conv_ln_b, w_o, norm_ffn_g,
              ff_wg, ff_wu, ff_wd, router_w, moe_wg, moe_wu, moe_wd, ple_norm_g, ple_gate_w, ple_proj_w,
              final_norm_g):
    y_prompt, k_prompt, v_prompt, conv_prompt = run_trunk(
        x_prompt, p_prompt, None, None, None, norm_mix_g, w_in, lam_q1, lam_k1, lam_q2, lam_k2,
        subln_g, dw_w, dw_b, conv_ln_g, conv_ln_b, w_o, norm_ffn_g, ff_wg, ff_wu, ff_wd,
        router_w, moe_wg, moe_wu, moe_wd, ple_norm_g, ple_gate_w, ple_proj_w, final_norm_g)
    y_sample, k_sample, v_sample, conv_sample = run_trunk(
        x_sample, p_sample, cache_k, cache_v, state_conv, norm_mix_g, w_in, lam_q1, lam_k1, lam_q2, lam_k2,
        subln_g, dw_w, dw_b, conv_ln_g, conv_ln_b, w_o, norm_ffn_g, ff_wg, ff_wu, ff_wd,
        router_w, moe_wg, moe_wu, moe_wd, ple_norm_g, ple_gate_w, ple_proj_w, final_norm_g)
    return (y_prompt, y_sample, k_prompt, v_prompt, conv_prompt, k_sample, v_sample, conv_sample)
```

```python
import functools
import math

import jax
import jax.numpy as jnp
from jax import lax
from jax.experimental import pallas as pl
from jax.experimental.pallas import tpu as pltpu

F32 = jnp.float32
BF16 = jnp.bfloat16

EPS = 1e-6
CHUNK = 64
N_HEADS = 4
DW_WIDTH = 31
CONV_BUF = DW_WIDTH - 1
HALO = 32
V7X_VMEM_LIMIT = 56 * 1024 * 1024


def _params(*sem):
    return pltpu.CompilerParams(dimension_semantics=sem, vmem_limit_bytes=V7X_VMEM_LIMIT)


def _tile(n, pref):
    if n <= pref:
        return n
    t = pref
    while n % t or t % 8:
        t -= 1
    return t


def _rms(x, g):
    return x * lax.rsqrt(jnp.mean(x * x, axis=-1, keepdims=True) + EPS) * g


def _dot(a, b):
    return jnp.dot(a, b, preferred_element_type=F32)


def _dot_nt(a, b):
    return lax.dot_general(a, b, (((1,), (1,)), ((), ())), preferred_element_type=F32)


def _resident(shape, index_map):
    return pl.BlockSpec(shape, index_map, pipeline_mode=pl.Buffered(1))


def _in_proj_kernel(h_ref, g_ref, w_ref, q_ref, kf_ref, vf_ref, kb_ref, vb_ref, u_ref, *, aw, cw):
    xn = _rms(h_ref[...], g_ref[...]).astype(BF16)

    def mm(lo, width):
        return _dot(xn, w_ref[:, lo:lo + width])

    q_ref[...] = mm(0, aw).astype(BF16)
    k = mm(aw, aw)
    kf_ref[...] = k
    kb_ref[...] = k.astype(BF16)
    v = mm(2 * aw, aw)
    vf_ref[...] = v
    vb_ref[...] = v.astype(BF16)
    ga = mm(3 * aw, cw)
    gb = mm(3 * aw + cw, cw)
    u_ref[...] = ga * jax.nn.sigmoid(gb)


def _in_proj(h, g, w_in_b, layer, aw, cw):
    n, d = h.shape
    tm = _tile(n, 512)
    row = lambda width: pl.BlockSpec((tm, width), lambda r: (r, 0))
    return pl.pallas_call(
        functools.partial(_in_proj_kernel, aw=aw, cw=cw),
        grid=(n // tm,),
        in_specs=[row(d),
                  _resident((1, d), lambda r: (0, 0)),
                  _resident((None, d, w_in_b.shape[-1]), lambda r: (layer, 0, 0))],
        out_specs=[row(aw), row(aw), row(aw), row(aw), row(aw), row(cw)],
        out_shape=[jax.ShapeDtypeStruct((n, aw), BF16),
                   jax.ShapeDtypeStruct((n, aw), F32),
                   jax.ShapeDtypeStruct((n, aw), F32),
                   jax.ShapeDtypeStruct((n, aw), BF16),
                   jax.ShapeDtypeStruct((n, aw), BF16),
                   jax.ShapeDtypeStruct((n, cw), F32)],
        compiler_params=_params("parallel"),
        name="in_proj",
    )(h, g.reshape(1, d), w_in_b)


def _split_halves(q):
    dk = q.shape[-1] // 2
    lane = lax.broadcasted_iota(jnp.int32, q.shape, 1)
    zero = jnp.zeros_like(q)
    return jnp.concatenate([jnp.where(lane < dk, q, zero), jnp.where(lane >= dk, q, zero)], axis=0)


def _lam(lam_ref, lam_init):
    lv = lam_ref[...]
    a1 = jnp.sum(lv[0:1] * lv[1:2], axis=-1, keepdims=True)
    a2 = jnp.sum(lv[2:3] * lv[3:4], axis=-1, keepdims=True)
    return jnp.exp(a1) - jnp.exp(a2) + lam_init


def _attn_finish(acc, l, lam, g, lam_init, t):
    oo = acc * (1.0 / l)
    o = oo[:t] - lam * oo[t:]
    return (_rms(o, g) * (1.0 - lam_init)).astype(BF16)


def _attn_prompt_kernel(lam_ref, q_ref, k_ref, v_ref, g_ref, o_ref, m_scr, l_scr, acc_scr, *, qb, lam_init):
    qi = pl.program_id(2)
    qq = _split_halves(q_ref[...])
    scale = (q_ref.shape[-1] // 2) ** -0.5
    m_scr[...] = jnp.full(m_scr.shape, -jnp.inf, F32)
    l_scr[...] = jnp.zeros(l_scr.shape, F32)
    acc_scr[...] = jnp.zeros(acc_scr.shape, F32)

    def step(j, masked):
        off = pl.multiple_of(j * qb, qb)
        s = _dot_nt(qq, k_ref[pl.ds(off, qb), :]) * scale
        if masked:
            row = lax.broadcasted_iota(jnp.int32, s.shape, 0)
            col = lax.broadcasted_iota(jnp.int32, s.shape, 1)
            s = jnp.where(col // CHUNK <= (row % qb) // CHUNK, s, -jnp.inf)
        m_prev = m_scr[...]
        m_new = jnp.maximum(m_prev, jnp.max(s, axis=-1, keepdims=True))
        alpha = jnp.exp(m_prev - m_new)
        p = jnp.exp(s - m_new)
        l_scr[...] = alpha * l_scr[...] + jnp.sum(p, axis=-1, keepdims=True)
        acc_scr[...] = alpha * acc_scr[...] + _dot(p.astype(BF16), v_ref[pl.ds(off, qb), :])
        m_scr[...] = m_new

    def full_step(j, carry):
        step(j, False)
        return carry

    lax.fori_loop(0, qi, full_step, 0)
    step(qi, True)
    o_ref[...] = _attn_finish(acc_scr[...], l_scr[...], _lam(lam_ref, lam_init), g_ref[...], lam_init, qb)


def _attn_prompt(q, k, v, lam_vecs, subln_g, lam_init, batch, seq):
    aw = q.shape[-1]
    hv = aw // N_HEADS
    qb = _tile(seq, 512)
    q3, k3, v3 = (a.reshape(batch, seq, aw) for a in (q, k, v))
    qspec = pl.BlockSpec((None, qb, hv), lambda b, h, i: (b, i, h))
    kvspec = pl.BlockSpec((None, seq, hv), lambda b, h, i: (b, 0, h))
    out = pl.pallas_call(
        functools.partial(_attn_prompt_kernel, qb=qb, lam_init=lam_init),
        grid=(batch, N_HEADS, seq // qb),
        in_specs=[_resident(lam_vecs.shape, lambda b, h, i: (0, 0)),
                  qspec, kvspec, kvspec,
                  _resident((1, hv), lambda b, h, i: (0, 0))],
        out_specs=qspec,
        out_shape=jax.ShapeDtypeStruct((batch, seq, aw), BF16),
        scratch_shapes=[pltpu.VMEM((2 * qb, 1), F32), pltpu.VMEM((2 * qb, 1), F32),
                        pltpu.VMEM((2 * qb, hv), F32)],
        compiler_params=_params("parallel", "parallel", "arbitrary"),
        name="attn_prompt",
    )(lam_vecs, q3, k3, v3, subln_g.reshape(1, hv))
    return out.reshape(batch * seq, aw)


def _attn_sample_kernel(lam_ref, q_ref, kc_ref, vc_ref, kn_ref, vn_ref, g_ref, o_ref, *, lam_init):
    t = q_ref.shape[0]
    qq = _split_halves(q_ref[...])
    scale = (q_ref.shape[-1] // 2) ** -0.5
    s_c = _dot_nt(qq, kc_ref[...].astype(BF16)) * scale
    s_n = _dot_nt(qq, kn_ref[...]) * scale
    m = jnp.maximum(jnp.max(s_c, axis=-1, keepdims=True), jnp.max(s_n, axis=-1, keepdims=True))
    e_c = jnp.exp(s_c - m)
    e_n = jnp.exp(s_n - m)
    l = jnp.sum(e_c, axis=-1, keepdims=True) + jnp.sum(e_n, axis=-1, keepdims=True)
    acc = _dot(e_c.astype(BF16), vc_ref[...].astype(BF16)) + _dot(e_n.astype(BF16), vn_ref[...])
    o_ref[...] = _attn_finish(acc, l, _lam(lam_ref, lam_init), g_ref[...], lam_init, t)


def _attn_sample(q, k, v, cache_k4, cache_v4, layer, lam_vecs, subln_g, lam_init, batch, seq):
    aw = q.shape[-1]
    hv = aw // N_HEADS
    past = cache_k4.shape[2]
    q3, k3, v3 = (a.reshape(batch, seq, aw) for a in (q, k, v))
    new = pl.BlockSpec((None, seq, hv), lambda b, h: (b, 0, h))
    cache = pl.BlockSpec((None, None, past, hv), lambda b, h: (layer, b, 0, h))
    out = pl.pallas_call(
        functools.partial(_attn_sample_kernel, lam_init=lam_init),
        grid=(batch, N_HEADS),
        in_specs=[_resident(lam_vecs.shape, lambda b, h: (0, 0)),
                  new, cache, cache, new, new,
                  _resident((1, hv), lambda b, h: (0, 0))],
        out_specs=new,
        out_shape=jax.ShapeDtypeStruct((batch, seq, aw), BF16),
        compiler_params=_params("parallel", "parallel"),
        name="attn_sample",
    )(lam_vecs, q3, cache_k4, cache_v4, k3, v3, subln_g.reshape(1, hv))
    return out.reshape(batch * seq, aw)


CONV_ROWS = 32


def _conv_kernel(*refs, tt, has_hist, multi_tile):
    refs = list(refs)
    u_ref = refs.pop(0)
    prev_ref = refs.pop(0) if multi_tile else None
    hist_ref = refs.pop(0) if has_hist else None
    w_ref, b_ref, lg_ref, lb_ref, c_ref, ext, win = refs
    cw = u_ref.shape[-1]
    t = pl.program_id(1)

    def first_tile_halo():
        ext[0:HALO, :] = jnp.zeros((HALO, cw), F32)
        if has_hist:
            ext[HALO - CONV_BUF:HALO, :] = hist_ref[...]

    if multi_tile:
        pl.when(t == 0)(first_tile_halo)

        @pl.when(t > 0)
        def _():
            ext[0:HALO, :] = prev_ref[...]
    else:
        first_tile_halo()
    ext[HALO:HALO + tt, :] = u_ref[...]

    bias = b_ref[...]
    lg = lg_ref[...]
    lb = lb_ref[...]
    base = HALO - CONV_BUF

    def group(r, carry):
        r0 = pl.multiple_of(r * CONV_ROWS, CONV_ROWS)
        win[...] = ext[pl.ds(r0, HALO + CONV_ROWS), :]
        acc = jnp.broadcast_to(bias, (CONV_ROWS, cw))
        for j in range(DW_WIDTH):
            acc = acc + w_ref[j:j + 1, :] * win[base + j:base + j + CONV_ROWS, :]
        mu = jnp.mean(acc, axis=-1, keepdims=True)
        xc = acc - mu
        var = jnp.mean(xc * xc, axis=-1, keepdims=True)
        y = xc * lax.rsqrt(var + EPS) * lg + lb
        c_ref[pl.ds(r0, CONV_ROWS), :] = (y * jax.nn.sigmoid(y)).astype(BF16)
        return carry

    lax.fori_loop(0, tt // CONV_ROWS, group, 0)


def _conv_branch(u, hist4, layer, dw_w, dw_b, ln_g, ln_b, batch, seq):
    cw = u.shape[-1]
    tt = _tile(seq, 512)
    nt = seq // tt
    multi_tile = nt > 1
    has_hist = hist4 is not None
    u3 = u.reshape(batch, seq, cw)
    args = [u3]
    in_specs = [pl.BlockSpec((None, tt, cw), lambda b, t: (b, t, 0))]
    if multi_tile:
        per = tt // HALO
        args.append(u3)
        in_specs.append(pl.BlockSpec((None, HALO, cw), lambda b, t: (b, jnp.maximum(t * per - 1, 0), 0)))
    if has_hist:
        args.append(hist4)
        in_specs.append(pl.BlockSpec((None, None, CONV_BUF, cw), lambda b, t: (layer, b, 0, 0)))
    vec = _resident((1, cw), lambda b, t: (0, 0))
    args += [dw_w, dw_b.reshape(1, cw), ln_g.reshape(1, cw), ln_b.reshape(1, cw)]
    in_specs += [_resident((DW_WIDTH, cw), lambda b, t: (0, 0)), vec, vec, vec]
    out = pl.pallas_call(
        functools.partial(_conv_kernel, tt=tt, has_hist=has_hist, multi_tile=multi_tile),
        grid=(batch, nt),
        in_specs=in_specs,
        out_specs=pl.BlockSpec((None, tt, cw), lambda b, t: (b, t, 0)),
        out_shape=jax.ShapeDtypeStruct((batch, seq, cw), BF16),
        scratch_shapes=[pltpu.VMEM((HALO + tt, cw), F32), pltpu.VMEM((HALO + CONV_ROWS, cw), F32)],
        compiler_params=_params("parallel", "arbitrary"),
        name="conv_branch",
    )(*args)
    return out.reshape(batch * seq, cw)


def _out_proj_kernel(h_ref, o_ref, c_ref, w_ref, out_ref, *, aw):
    out_ref[...] = h_ref[...] + _dot(o_ref[...], w_ref[0:aw, :]) + _dot(c_ref[...], w_ref[aw:, :])


def _out_proj(h, o, c, w_o_b, layer):
    n, d = h.shape
    aw = o.shape[-1]
    tm = _tile(n, 512)
    row = lambda width: pl.BlockSpec((tm, width), lambda r: (r, 0))
    return pl.pallas_call(
        functools.partial(_out_proj_kernel, aw=aw),
        grid=(n // tm,),
        in_specs=[row(d), row(aw), row(c.shape[-1]), _resident((None, d, d), lambda r: (layer, 0, 0))],
        out_specs=row(d),
        out_shape=jax.ShapeDtypeStruct((n, d), F32),
        compiler_params=_params("parallel"),
        name="out_proj",
    )(h, o, c, w_o_b)


def _swiglu_chunk(xn, wg, wu, wd):
    g = _dot(xn, wg)
    u = _dot(xn, wu)
    return _dot((g * jax.nn.sigmoid(g) * u).astype(BF16), wd)


def _ffn_dense_kernel(h_ref, g_ref, wg_ref, wu_ref, wd_ref, out_ref, *, fc):
    x = h_ref[...]
    xn = _rms(x, g_ref[...]).astype(BF16)
    acc = x
    for lo in range(0, wg_ref.shape[-1], fc):
        acc = acc + _swiglu_chunk(xn, wg_ref[:, lo:lo + fc], wu_ref[:, lo:lo + fc], wd_ref[lo:lo + fc, :])
    out_ref[...] = acc


def _ffn_dense(h, g, wg_b, wu_b, wd_b, j):
    n, d = h.shape
    f = wg_b.shape[-1]
    tm = _tile(n, 512)
    fc = f // 2
    row = pl.BlockSpec((tm, d), lambda r: (r, 0))
    return pl.pallas_call(
        functools.partial(_ffn_dense_kernel, fc=fc),
        grid=(n // tm,),
        in_specs=[row, _resident((1, d), lambda r: (0, 0)),
                  _resident((None, d, f), lambda r: (j, 0, 0)),
                  _resident((None, d, f), lambda r: (j, 0, 0)),
                  _resident((None, f, d), lambda r: (j, 0, 0))],
        out_specs=row,
        out_shape=jax.ShapeDtypeStruct((n, d), F32),
        compiler_params=_params("parallel"),
        name="ffn_dense",
    )(h, g.reshape(1, d), wg_b, wu_b, wd_b)


ROUTER_LANES = 128


def _top2_gates(logits, n_experts):
    lane = lax.broadcasted_iota(jnp.int32, logits.shape, 1)
    neg = jnp.float32(-jnp.inf)
    lg = jnp.where(lane < n_experts, logits, neg)
    m1 = jnp.max(lg, axis=-1, keepdims=True)
    i1 = jnp.min(jnp.where(lg == m1, lane, ROUTER_LANES), axis=-1, keepdims=True)
    lg2 = jnp.where(lane == i1, neg, lg)
    m2 = jnp.max(lg2, axis=-1, keepdims=True)
    i2 = jnp.min(jnp.where(lg2 == m2, lane, ROUTER_LANES), axis=-1, keepdims=True)
    e2 = jnp.exp(m2 - m1)
    den = 1.0 + e2
    return jnp.where(lane == i1, 1.0 / den, 0.0) + jnp.where(lane == i2, e2 / den, 0.0)


def _ffn_moe_kernel(h_ref, g_ref, rw_ref, wg_ref, wu_ref, wd_ref, out_ref, xn_scr, comb_scr, *, n_experts):
    e = pl.program_id(1)
    c = pl.program_id(2)

    @pl.when((e == 0) & (c == 0))
    def _():
        x = h_ref[...]
        xn = _rms(x, g_ref[...]).astype(BF16)
        xn_scr[...] = xn
        comb_scr[...] = _top2_gates(_dot(xn, rw_ref[...]), n_experts)
        out_ref[...] = x

    lane = lax.broadcasted_iota(jnp.int32, comb_scr.shape, 1)
    gate = jnp.sum(jnp.where(lane == e, comb_scr[...], 0.0), axis=-1, keepdims=True)
    out_ref[...] += gate * _swiglu_chunk(xn_scr[...], wg_ref[...], wu_ref[...], wd_ref[...])


def _ffn_moe(h, g, router_b, wg_b, wu_b, wd_b, j):
    n, d = h.shape
    n_experts, f = wg_b.shape[1], wg_b.shape[-1]
    tm = _tile(n, 1024)
    fc = f // 4
    row = pl.BlockSpec((tm, d), lambda r, e, c: (r, 0))
    return pl.pallas_call(
        functools.partial(_ffn_moe_kernel, n_experts=n_experts),
        grid=(n // tm, n_experts, f // fc),
        in_specs=[row, _resident((1, d), lambda r, e, c: (0, 0)),
                  _resident((None, d, ROUTER_LANES), lambda r, e, c: (j, 0, 0)),
                  pl.BlockSpec((None, None, d, fc), lambda r, e, c: (j, e, 0, c)),
                  pl.BlockSpec((None, None, d, fc), lambda r, e, c: (j, e, 0, c)),
                  pl.BlockSpec((None, None, fc, d), lambda r, e, c: (j, e, c, 0))],
        out_specs=row,
        out_shape=jax.ShapeDtypeStruct((n, d), F32),
        scratch_shapes=[pltpu.VMEM((tm, d), BF16), pltpu.VMEM((tm, ROUTER_LANES), F32)],
        compiler_params=_params("parallel", "arbitrary", "arbitrary"),
        name="ffn_moe",
    )(h, g.reshape(1, d), router_b, wg_b, wu_b, wd_b)


def _ple_kernel(*refs, final):
    if final:
        h_ref, p_ref, g_ref, wg_ref, wp_ref, fg_ref, out_ref = refs
    else:
        h_ref, p_ref, g_ref, wg_ref, wp_ref, out_ref = refs
    x = h_ref[...]
    gate = jax.nn.sigmoid(_dot(_rms(x, g_ref[...]).astype(BF16), wg_ref[...]))
    h3 = x + gate * _dot(p_ref[...].astype(BF16), wp_ref[...])
    out_ref[...] = _rms(h3, fg_ref[...]) if final else h3


def _ple(h, p3, layer, g, gate_w_b, proj_w_b, final_g):
    n, d = h.shape
    dp = p3.shape[-1]
    tm = _tile(n, 512)
    final = final_g is not None
    row = pl.BlockSpec((tm, d), lambda r: (r, 0))
    vec = _resident((1, d), lambda r: (0, 0))
    args = [h, p3, g.reshape(1, d), gate_w_b, proj_w_b]
    in_specs = [row, pl.BlockSpec((None, tm, dp), lambda r: (layer, r, 0)), vec,
                _resident((None, d, d), lambda r: (layer, 0, 0)),
                _resident((None, dp, d), lambda r: (layer, 0, 0))]
    if final:
        args.append(final_g.reshape(1, d))
        in_specs.append(vec)
    return pl.pallas_call(
        functools.partial(_ple_kernel, final=final),
        grid=(n // tm,),
        in_specs=in_specs,
        out_specs=row,
        out_shape=jax.ShapeDtypeStruct((n, d), F32),
        compiler_params=_params("parallel"),
        name="ple",
    )(*args)


def _trunk(x, p, cache_k, cache_v, state_conv, wts):
    batch, seq, d = x.shape
    depth = wts["w_in"].shape[0]
    aw = d // 2
    cw = d - aw
    n = batch * seq
    h = x.reshape(n, d)
    p3 = p.reshape(depth, n, p.shape[-1])
    if cache_k is not None:
        cache_k4 = cache_k.reshape(cache_k.shape[:3] + (aw,))
        cache_v4 = cache_v.reshape(cache_v.shape[:3] + (aw,))
    ks, vs, cs = [], [], []
    for i in range(depth):
        lam_init = 0.8 - 0.6 * math.exp(-0.3 * i)
        lam_vecs = jnp.stack([wts["lam_q1"][i], wts["lam_k1"][i], wts["lam_q2"][i], wts["lam_k2"][i]])
        q, kf, vf, kb, vb, u = _in_proj(h, wts["norm_mix_g"][i], wts["w_in"], i, aw, cw)
        if cache_k is None:
            o = _attn_prompt(q, kb, vb, lam_vecs, wts["subln_g"][i], lam_init, batch, seq)
            hist = None
        else:
            o = _attn_sample(q, kb, vb, cache_k4, cache_v4, i, lam_vecs, wts["subln_g"][i], lam_init, batch, seq)
            hist = state_conv
        c = _conv_branch(u, hist, i, wts["dw_w"][i], wts["dw_b"][i], wts["conv_ln_g"][i], wts["conv_ln_b"][i],
                         batch, seq)
        h = _out_proj(h, o, c, wts["w_o"], i)
        j = i // 2
        if i % 2 == 0:
            h = _ffn_dense(h, wts["norm_ffn_g"][i], wts["ff_wg"], wts["ff_wu"], wts["ff_wd"], j)
        else:
            h = _ffn_moe(h, wts["norm_ffn_g"][i], wts["router_w"], wts["moe_wg"], wts["moe_wu"], wts["moe_wd"], j)
        h = _ple(h, p3, i, wts["ple_norm_g"][i], wts["ple_gate_w"], wts["ple_proj_w"],
                 wts["final_norm_g"] if i == depth - 1 else None)
        ks.append(kf.reshape(batch, seq, N_HEADS, aw // N_HEADS))
        vs.append(vf.reshape(batch, seq, N_HEADS, aw // N_HEADS))
        cs.append(u.reshape(batch, seq, cw)[:, seq - CONV_BUF:])
    return h.reshape(batch, seq, d), jnp.stack(ks), jnp.stack(vs), jnp.stack(cs)


def kernel(x_prompt, x_sample, cache_k, cache_v, state_conv, p_prompt, p_sample, norm_mix_g, w_in, lam_q1, lam_k1, lam_q2, lam_k2, subln_g, dw_w, dw_b, conv_ln_g, conv_ln_b, w_o, norm_ffn_g, ff_wg, ff_wu, ff_wd, router_w, moe_wg, moe_wu, moe_wd, ple_norm_g, ple_gate_w, ple_proj_w, final_norm_g):
    n_experts = router_w.shape[-1]
    wts = dict(
        norm_mix_g=norm_mix_g, lam_q1=lam_q1, lam_k1=lam_k1, lam_q2=lam_q2, lam_k2=lam_k2, subln_g=subln_g,
        dw_w=dw_w, dw_b=dw_b, conv_ln_g=conv_ln_g, conv_ln_b=conv_ln_b, norm_ffn_g=norm_ffn_g,
        ple_norm_g=ple_norm_g, final_norm_g=final_norm_g,
        w_in=w_in.astype(BF16), w_o=w_o.astype(BF16),
        ff_wg=ff_wg.astype(BF16), ff_wu=ff_wu.astype(BF16), ff_wd=ff_wd.astype(BF16),
        router_w=jnp.pad(router_w, ((0, 0), (0, 0), (0, ROUTER_LANES - n_experts))).astype(BF16),
        moe_wg=moe_wg.astype(BF16), moe_wu=moe_wu.astype(BF16), moe_wd=moe_wd.astype(BF16),
        ple_gate_w=ple_gate_w.astype(BF16), ple_proj_w=ple_proj_w.astype(BF16),
    )
    y_p, k_p, v_p, c_p = _trunk(x_prompt, p_prompt, None, None, None, wts)
    y_s, k_s, v_s, c_s = _trunk(x_sample, p_sample, cache_k, cache_v, state_conv, wts)
    return (y_p, y_s, k_p, v_p, c_p, k_s, v_s, c_s)
```

```python
import functools
import math

import jax
import jax.numpy as jnp
from jax import lax
from jax.experimental import pallas as pl
from jax.experimental.pallas import tpu as pltpu

F32 = jnp.float32
BF16 = jnp.bfloat16

EPS = 1e-6
LOG2E = math.log2(math.e)
CHUNK = 64
N_HEADS = 4
DW_WIDTH = 31
CONV_BUF = DW_WIDTH - 1
HALO = 32
SUBLANES = 8
V7X_MXU_DIM = 256
V7X_VMEM_LIMIT = 56 * 1024 * 1024


def _params(*sem):
    return pltpu.CompilerParams(dimension_semantics=sem, vmem_limit_bytes=V7X_VMEM_LIMIT)


def _tile(n, pref):
    if n <= pref:
        return n
    t = pref
    while n % t or t % 8:
        t -= 1
    return t


def _rms(x, g):
    return x * lax.rsqrt(jnp.mean(x * x, axis=-1, keepdims=True) + EPS) * g


def _dot(a, b):
    return jnp.dot(a, b, preferred_element_type=F32)


def _dot_nt(a, b):
    return lax.dot_general(a, b, (((1,), (1,)), ((), ())), preferred_element_type=F32)


def _resident(shape, index_map):
    return pl.BlockSpec(shape, index_map, pipeline_mode=pl.Buffered(1))


def _in_proj_kernel(h_ref, g_ref, w_ref, q_ref, kf_ref, vf_ref, kb_ref, vb_ref, u_ref, *, aw, cw, q_scale,
                    transpose_v):
    xn = _rms(h_ref[...], g_ref[...]).astype(BF16)

    def mm(lo, width):
        return _dot(xn, w_ref[:, lo:lo + width])

    q_ref[...] = (mm(0, aw) * q_scale).astype(BF16)
    k = mm(aw, aw)
    kf_ref[...] = k
    kb_ref[...] = k.astype(BF16)
    v = mm(2 * aw, aw)
    vf_ref[...] = v
    vb_ref[...] = (v.T if transpose_v else v).astype(BF16)
    ga = mm(3 * aw, cw)
    gb = mm(3 * aw + cw, cw)
    u_ref[...] = ga * jax.nn.sigmoid(gb)


def _in_proj(h, g, w_in_b, layer, aw, cw, seq, transpose_v):
    n, d = h.shape
    tm = _tile(seq, 512)
    per = seq // tm
    q_scale = (aw // N_HEADS // 2) ** -0.5 * LOG2E
    row = lambda width: pl.BlockSpec((tm, width), lambda r: (r, 0))
    if transpose_v:
        vb_spec = pl.BlockSpec((None, aw, tm), lambda r: (r // per, 0, r % per))
        vb_shape = jax.ShapeDtypeStruct((n // seq, aw, seq), BF16)
    else:
        vb_spec, vb_shape = row(aw), jax.ShapeDtypeStruct((n, aw), BF16)
    return pl.pallas_call(
        functools.partial(_in_proj_kernel, aw=aw, cw=cw, q_scale=q_scale, transpose_v=transpose_v),
        grid=(n // tm,),
        in_specs=[row(d),
                  _resident((1, d), lambda r: (0, 0)),
                  _resident((None, d, w_in_b.shape[-1]), lambda r: (layer, 0, 0))],
        out_specs=[row(aw), row(aw), row(aw), row(aw), vb_spec, row(cw)],
        out_shape=[jax.ShapeDtypeStruct((n, aw), BF16),
                   jax.ShapeDtypeStruct((n, aw), F32),
                   jax.ShapeDtypeStruct((n, aw), F32),
                   jax.ShapeDtypeStruct((n, aw), BF16),
                   vb_shape,
                   jax.ShapeDtypeStruct((n, cw), F32)],
        compiler_params=_params("parallel"),
        name="in_proj",
    )(h, g.reshape(1, d), w_in_b)


def _split_halves(q):
    dk = q.shape[-1] // 2
    lane = lax.broadcasted_iota(jnp.int32, q.shape, 1)
    zero = jnp.zeros_like(q)
    return jnp.concatenate([jnp.where(lane < dk, q, zero), jnp.where(lane >= dk, q, zero)], axis=0)


def _lam(lam_ref, lam_init):
    lv = lam_ref[...]
    a1 = jnp.sum(lv[0:1] * lv[1:2], axis=-1, keepdims=True)
    a2 = jnp.sum(lv[2:3] * lv[3:4], axis=-1, keepdims=True)
    return jnp.exp(a1) - jnp.exp(a2) + lam_init


ATTN_KC = V7X_MXU_DIM
ATTN_HEADS = 2


def _attn_prompt_kernel(lam_ref, q_ref, k_ref, vt_ref, g_ref, o_ref, qt_scr, sa_scr, sb_scr, m_scr, l_scr, acc_scr, *,
                        nh, lam_init):
    kc = ATTN_KC
    qb = kc
    qi = pl.program_id(2)
    hv = q_ref.shape[-1] // nh
    heads = range(nh)
    for h in heads:
        qq = _split_halves(q_ref[:, h * hv:(h + 1) * hv].astype(F32))
        qt_scr[h] = qq.T.astype(BF16)
    m_scr[...] = jnp.full(m_scr.shape, -jnp.inf, F32)
    l_scr[...] = jnp.zeros(l_scr.shape, F32)
    acc_scr[...] = jnp.zeros(acc_scr.shape, F32)

    def scores(c, s_ref):
        off = pl.multiple_of(c * kc, kc)
        for h in heads:
            s_ref[h] = _dot(k_ref[pl.ds(off, kc), h * hv:(h + 1) * hv], qt_scr[h])

    def consume(c, s_ref, masked):
        off = pl.multiple_of(c * kc, kc)
        for h in heads:
            s = s_ref[h]
            if masked:
                key = lax.broadcasted_iota(jnp.int32, s.shape, 0)
                col = lax.broadcasted_iota(jnp.int32, s.shape, 1)
                s = jnp.where(key // CHUNK <= (col % qb) // CHUNK, s, -jnp.inf)
            m_prev = m_scr[h]
            m_new = jnp.maximum(m_prev, jnp.max(s, axis=0, keepdims=True))
            alpha = jnp.exp2(m_prev - m_new)
            p = jnp.exp2(s - m_new)
            l_scr[h] = alpha * l_scr[h] + jnp.sum(p, axis=0, keepdims=True)
            acc_scr[h] = alpha * acc_scr[h] + _dot(vt_ref[h * hv:(h + 1) * hv, pl.ds(off, kc)], p.astype(BF16))
            m_scr[h] = m_new

    scores(0, sa_scr)

    def pair(t, carry):
        c = 2 * t
        scores(c + 1, sb_scr)
        consume(c, sa_scr, False)
        scores(c + 2, sa_scr)
        consume(c + 1, sb_scr, False)
        return carry

    lax.fori_loop(0, qi // 2, pair, 0)
    c0 = 2 * (qi // 2)

    @pl.when(qi % 2 == 1)
    def _():
        scores(c0 + 1, sb_scr)
        consume(c0, sa_scr, False)
        consume(c0 + 1, sb_scr, True)

    @pl.when(qi % 2 == 0)
    def _():
        consume(c0, sa_scr, True)

    lam = _lam(lam_ref, lam_init)
    for h in heads:
        oo = acc_scr[h] * (1.0 / l_scr[h])
        o = (oo[:, :qb] - lam * oo[:, qb:]).T
        o_ref[:, h * hv:(h + 1) * hv] = (_rms(o, g_ref[...]) * (1.0 - lam_init)).astype(BF16)


def _attn_prompt(q, k, vt3, lam_vecs, subln_g, lam_init, batch, seq):
    aw = q.shape[-1]
    hv = aw // N_HEADS
    nh = ATTN_HEADS
    qb = kc = ATTN_KC
    assert seq % qb == 0 and qb % CHUNK == 0 and N_HEADS % nh == 0
    q3, k3 = (a.reshape(batch, seq, aw) for a in (q, k))
    qspec = pl.BlockSpec((None, qb, nh * hv), lambda b, h, i: (b, i, h))
    out = pl.pallas_call(
        functools.partial(_attn_prompt_kernel, nh=nh, lam_init=lam_init),
        grid=(batch, N_HEADS // nh, seq // qb),
        in_specs=[_resident(lam_vecs.shape, lambda b, h, i: (0, 0)),
                  qspec,
                  pl.BlockSpec((None, seq, nh * hv), lambda b, h, i: (b, 0, h)),
                  pl.BlockSpec((None, nh * hv, seq), lambda b, h, i: (b, h, 0)),
                  _resident((1, hv), lambda b, h, i: (0, 0))],
        out_specs=qspec,
        out_shape=jax.ShapeDtypeStruct((batch, seq, aw), BF16),
        scratch_shapes=[pltpu.VMEM((nh, hv, 2 * qb), BF16),
                        pltpu.VMEM((nh, kc, 2 * qb), F32), pltpu.VMEM((nh, kc, 2 * qb), F32),
                        pltpu.VMEM((nh, 1, 2 * qb), F32), pltpu.VMEM((nh, 1, 2 * qb), F32),
                        pltpu.VMEM((nh, hv, 2 * qb), F32)],
        compiler_params=_params("parallel", "parallel", "arbitrary"),
        name="attn_prompt",
    )(lam_vecs, q3, k3, vt3, subln_g.reshape(1, hv))
    return out.reshape(batch * seq, aw)


def _attn_sample_kernel(lam_ref, q_ref, kc_ref, vc_ref, kn_ref, vn_ref, g_ref, o_ref, *, lam_init):
    t = q_ref.shape[0]
    qq = _split_halves(q_ref[...])
    s_c = _dot_nt(qq, kc_ref[...].astype(BF16))
    s_n = _dot_nt(qq, kn_ref[...])
    m = jnp.maximum(jnp.max(s_c, axis=-1, keepdims=True), jnp.max(s_n, axis=-1, keepdims=True))
    e_c = jnp.exp2(s_c - m)
    e_n = jnp.exp2(s_n - m)
    l = jnp.sum(e_c, axis=-1, keepdims=True) + jnp.sum(e_n, axis=-1, keepdims=True)
    acc = _dot(e_c.astype(BF16), vc_ref[...].astype(BF16)) + _dot(e_n.astype(BF16), vn_ref[...])
    oo = acc * (1.0 / l)
    o = oo[:t] - _lam(lam_ref, lam_init) * oo[t:]
    o_ref[...] = (_rms(o, g_ref[...]) * (1.0 - lam_init)).astype(BF16)


def _attn_sample(q, k, v, cache_k4, cache_v4, layer, lam_vecs, subln_g, lam_init, batch, seq):
    aw = q.shape[-1]
    hv = aw // N_HEADS
    past = cache_k4.shape[2]
    q3, k3, v3 = (a.reshape(batch, seq, aw) for a in (q, k, v))
    new = pl.BlockSpec((None, seq, hv), lambda b, h: (b, 0, h))
    cache = pl.BlockSpec((None, None, past, hv), lambda b, h: (layer, b, 0, h))
    out = pl.pallas_call(
        functools.partial(_attn_sample_kernel, lam_init=lam_init),
        grid=(batch, N_HEADS),
        in_specs=[_resident(lam_vecs.shape, lambda b, h: (0, 0)),
                  new, cache, cache, new, new,
                  _resident((1, hv), lambda b, h: (0, 0))],
        out_specs=new,
        out_shape=jax.ShapeDtypeStruct((batch, seq, aw), BF16),
        compiler_params=_params("parallel", "parallel"),
        name="attn_sample",
    )(lam_vecs, q3, cache_k4, cache_v4, k3, v3, subln_g.reshape(1, hv))
    return out.reshape(batch * seq, aw)


CONV_ROWS = 32


def _conv_kernel(*refs, tt, has_hist, multi_tile):
    refs = list(refs)
    u_ref = refs.pop(0)
    prev_ref = refs.pop(0) if multi_tile else None
    hist_ref = refs.pop(0) if has_hist else None
    w_ref, b_ref, lg_ref, lb_ref, c_ref, ext = refs
    cw = u_ref.shape[-1]
    t = pl.program_id(1)

    def first_tile_halo():
        ext[0:HALO, :] = jnp.zeros((HALO, cw), F32)
        if has_hist:
            ext[HALO - CONV_BUF:HALO, :] = hist_ref[...]

    if multi_tile:
        pl.when(t == 0)(first_tile_halo)

        @pl.when(t > 0)
        def _():
            ext[0:HALO, :] = prev_ref[...]
    else:
        first_tile_halo()
    ext[HALO:HALO + tt, :] = u_ref[...]

    bias = b_ref[...]
    lg = lg_ref[...]
    lb = lb_ref[...]
    base = HALO - CONV_BUF

    def group(r, carry):
        r0 = pl.multiple_of(r * CONV_ROWS, CONV_ROWS)
        win = ext[pl.ds(r0, HALO + CONV_ROWS), :]
        acc = jnp.broadcast_to(bias, (CONV_ROWS, cw))
        nwin = HALO + CONV_ROWS
        for b in range(SUBLANES):
            shifted = pltpu.roll(win, (nwin - base - b) % nwin, axis=0)
            for j in range(b, DW_WIDTH, SUBLANES):
                acc = acc + w_ref[j:j + 1, :] * shifted[j - b:j - b + CONV_ROWS, :]
        mu = jnp.mean(acc, axis=-1, keepdims=True)
        xc = acc - mu
        var = jnp.mean(xc * xc, axis=-1, keepdims=True)
        y = xc * lax.rsqrt(var + EPS) * lg + lb
        c_ref[pl.ds(r0, CONV_ROWS), :] = (y * jax.nn.sigmoid(y)).astype(BF16)
        return carry

    lax.fori_loop(0, tt // CONV_ROWS, group, 0)


def _conv_branch(u, hist4, layer, dw_w, dw_b, ln_g, ln_b, batch, seq):
    cw = u.shape[-1]
    tt = _tile(seq, 512)
    nt = seq // tt
    multi_tile = nt > 1
    has_hist = hist4 is not None
    u3 = u.reshape(batch, seq, cw)
    args = [u3]
    in_specs = [pl.BlockSpec((None, tt, cw), lambda b, t: (b, t, 0))]
    if multi_tile:
        per = tt // HALO
        args.append(u3)
        in_specs.append(pl.BlockSpec((None, HALO, cw), lambda b, t: (b, jnp.maximum(t * per - 1, 0), 0)))
    if has_hist:
        args.append(hist4)
        in_specs.append(pl.BlockSpec((None, None, CONV_BUF, cw), lambda b, t: (layer, b, 0, 0)))
    vec = _resident((1, cw), lambda b, t: (0, 0))
    args += [dw_w, dw_b.reshape(1, cw), ln_g.reshape(1, cw), ln_b.reshape(1, cw)]
    in_specs += [_resident((DW_WIDTH, cw), lambda b, t: (0, 0)), vec, vec, vec]
    out = pl.pallas_call(
        functools.partial(_conv_kernel, tt=tt, has_hist=has_hist, multi_tile=multi_tile),
        grid=(batch, nt),
        in_specs=in_specs,
        out_specs=pl.BlockSpec((None, tt, cw), lambda b, t: (b, t, 0)),
        out_shape=jax.ShapeDtypeStruct((batch, seq, cw), BF16),
        scratch_shapes=[pltpu.VMEM((HALO + tt, cw), F32)],
        compiler_params=_params("parallel", "arbitrary"),
        name="conv_branch",
    )(*args)
    return out.reshape(batch * seq, cw)


def _out_proj_kernel(h_ref, o_ref, c_ref, w_ref, out_ref, *, aw):
    out_ref[...] = h_ref[...] + _dot(o_ref[...], w_ref[0:aw, :]) + _dot(c_ref[...], w_ref[aw:, :])


def _out_proj(h, o, c, w_o_b, layer):
    n, d = h.shape
    aw = o.shape[-1]
    tm = _tile(n, 512)
    row = lambda width: pl.BlockSpec((tm, width), lambda r: (r, 0))
    return pl.pallas_call(
        functools.partial(_out_proj_kernel, aw=aw),
        grid=(n // tm,),
        in_specs=[row(d), row(aw), row(c.shape[-1]), _resident((None, d, d), lambda r: (layer, 0, 0))],
        out_specs=row(d),
        out_shape=jax.ShapeDtypeStruct((n, d), F32),
        compiler_params=_params("parallel"),
        name="out_proj",
    )(h, o, c, w_o_b)


def _swiglu_chunk(xn, wg, wu, wd):
    g = _dot(xn, wg)
    u = _dot(xn, wu)
    return _dot((g * jax.nn.sigmoid(g) * u).astype(BF16), wd)


def _ffn_dense_kernel(h_ref, g_ref, wg_ref, wu_ref, wd_ref, out_ref, *, fc):
    x = h_ref[...]
    xn = _rms(x, g_ref[...]).astype(BF16)
    acc = x
    for lo in range(0, wg_ref.shape[-1], fc):
        acc = acc + _swiglu_chunk(xn, wg_ref[:, lo:lo + fc], wu_ref[:, lo:lo + fc], wd_ref[lo:lo + fc, :])
    out_ref[...] = acc


def _ffn_dense(h, g, wg_b, wu_b, wd_b, j):
    n, d = h.shape
    f = wg_b.shape[-1]
    tm = _tile(n, 512)
    fc = f // 2
    row = pl.BlockSpec((tm, d), lambda r: (r, 0))
    return pl.pallas_call(
        functools.partial(_ffn_dense_kernel, fc=fc),
        grid=(n // tm,),
        in_specs=[row, _resident((1, d), lambda r: (0, 0)),
                  _resident((None, d, f), lambda r: (j, 0, 0)),
                  _resident((None, d, f), lambda r: (j, 0, 0)),
                  _resident((None, f, d), lambda r: (j, 0, 0))],
        out_specs=row,
        out_shape=jax.ShapeDtypeStruct((n, d), F32),
        compiler_params=_params("parallel"),
        name="ffn_dense",
    )(h, g.reshape(1, d), wg_b, wu_b, wd_b)


ROUTER_LANES = 128
TOP_K = 2


def _router_kernel(h_ref, g_ref, rw_ref, idx_ref, gate_ref, *, n_experts):
    xn = _rms(h_ref[...], g_ref[...]).astype(BF16)
    logits = _dot(xn, rw_ref[...])
    lane = lax.broadcasted_iota(jnp.int32, logits.shape, 1)
    neg = jnp.float32(-jnp.inf)
    lg = jnp.where(lane < n_experts, logits, neg)
    m1 = jnp.max(lg, axis=-1, keepdims=True)
    i1 = jnp.min(jnp.where(lg == m1, lane, ROUTER_LANES), axis=-1, keepdims=True)
    lg2 = jnp.where(lane == i1, neg, lg)
    m2 = jnp.max(lg2, axis=-1, keepdims=True)
    i2 = jnp.min(jnp.where(lg2 == m2, lane, ROUTER_LANES), axis=-1, keepdims=True)
    e2 = jnp.exp(m2 - m1)
    den = 1.0 + e2
    idx_ref[...] = jnp.where(lane == 0, i1, jnp.where(lane == 1, i2, 0))
    gate_ref[...] = jnp.where(lane == 0, 1.0 / den, jnp.where(lane == 1, e2 / den, 0.0))


def _router(h, g, router_b, j, n_experts):
    n, d = h.shape
    tm = _tile(n, 512)
    lanes = pl.BlockSpec((tm, ROUTER_LANES), lambda r: (r, 0))
    return pl.pallas_call(
        functools.partial(_router_kernel, n_experts=n_experts),
        grid=(n // tm,),
        in_specs=[pl.BlockSpec((tm, d), lambda r: (r, 0)), _resident((1, d), lambda r: (0, 0)),
                  _resident((None, d, ROUTER_LANES), lambda r: (j, 0, 0))],
        out_specs=[lanes, lanes],
        out_shape=[jax.ShapeDtypeStruct((n, ROUTER_LANES), jnp.int32),
                   jax.ShapeDtypeStruct((n, ROUTER_LANES), F32)],
        compiler_params=_params("parallel"),
        name="router",
    )(h, g.reshape(1, d), router_b)


def _dispatch_plan(top_i, n_experts, tm):
    n = top_i.shape[0]
    n_slots = TOP_K * n
    n_tiles = -(-n_slots // tm) + n_experts
    n_pos = n_tiles * tm
    e_flat = top_i.reshape(-1)
    order = jnp.argsort(e_flat, stable=True).astype(jnp.int32)
    counts = jnp.sum((e_flat[:, None] == jnp.arange(n_experts, dtype=jnp.int32)[None, :]).astype(jnp.int32), axis=0)
    padded = -(-counts // tm) * tm
    ends = jnp.cumsum(padded)
    off = ends - padded
    coff = jnp.cumsum(counts) - counts
    pos = jnp.arange(n_pos, dtype=jnp.int32)
    e_of = jnp.minimum(jnp.searchsorted(ends, pos, side="right").astype(jnp.int32), n_experts - 1)
    r = pos - off[e_of]
    valid = (pos < ends[-1]) & (r < counts[e_of])
    slot = order[jnp.clip(coff[e_of] + r, 0, n_slots - 1)]
    tok = slot // TOP_K
    pad_rank = jnp.cumsum((~valid).astype(jnp.int32)) - 1
    src_tok = jnp.where(valid, tok, 0)
    dst_row = jnp.where(valid, (slot % TOP_K) * n + tok, n_slots + pad_rank)
    tile_pos = jnp.arange(n_tiles, dtype=jnp.int32) * tm
    tile_expert = e_of[tile_pos]
    tile_active = (tile_pos < ends[-1]).astype(jnp.int32)
    return src_tok.reshape(n_tiles, 1, tm), dst_row.reshape(n_tiles, 1, tm), tile_expert, tile_active


def _moe_kernel(te_ref, act_ref, src_ref, nsrc_ref, dst_ref, g_ref, h_hbm, wg_ref, wu_ref, wd_ref, y_hbm,
                xbuf, xn_scr, acc, gsem, ssem, *, tm):
    t = pl.program_id(0)
    c = pl.program_id(1)
    n_t = pl.num_programs(0)
    n_c = pl.num_programs(1)
    active = act_ref[t] == 1

    def gather(idx_ref):
        def issue(r, carry):
            pltpu.make_async_copy(h_hbm.at[pl.ds(idx_ref[0, r], 1)], xbuf.at[pl.ds(r, 1)], gsem).start()
            return carry
        lax.fori_loop(0, tm, issue, 0, unroll=8)

    def wait_gather():
        def one(r, carry):
            pltpu.make_async_copy(h_hbm.at[pl.ds(0, 1)], xbuf.at[pl.ds(r, 1)], gsem).wait()
            return carry
        lax.fori_loop(0, tm, one, 0, unroll=8)

    def scatter():
        def issue(r, carry):
            pltpu.make_async_copy(acc.at[pl.ds(r, 1)], y_hbm.at[pl.ds(dst_ref[0, r], 1)], ssem).start()
            return carry
        lax.fori_loop(0, tm, issue, 0, unroll=8)

    def wait_scatter():
        def one(r, carry):
            pltpu.make_async_copy(acc.at[pl.ds(r, 1)], y_hbm.at[pl.ds(0, 1)], ssem).wait()
            return carry
        lax.fori_loop(0, tm, one, 0, unroll=8)

    @pl.when((c == 0) & (t == 0) & active)
    def _():
        gather(src_ref)

    @pl.when((c == 0) & (t > 0))
    def _():
        @pl.when(act_ref[jnp.maximum(t - 1, 0)] == 1)
        def _():
            wait_scatter()

    @pl.when((c == 0) & active)
    def _():
        wait_gather()
        xn_scr[...] = _rms(xbuf[...], g_ref[...]).astype(BF16)

        @pl.when(act_ref[jnp.minimum(t + 1, n_t - 1)] * (t + 1 < n_t).astype(jnp.int32) == 1)
        def _():
            gather(nsrc_ref)

    @pl.when(active)
    def _():
        y = _swiglu_chunk(xn_scr[...], wg_ref[...], wu_ref[...], wd_ref[...])

        @pl.when(c == 0)
        def _():
            acc[...] = y

        @pl.when(c > 0)
        def _():
            acc[...] += y

    @pl.when((c == n_c - 1) & active)
    def _():
        scatter()

        @pl.when(t == n_t - 1)
        def _():
            wait_scatter()


def _ffn_moe(h, g, router_b, wg_b, wu_b, wd_b, j):
    n, d = h.shape
    n_experts, f = wg_b.shape[1], wg_b.shape[-1]
    idx, gates = _router(h, g, router_b, j, n_experts)
    tm = 1024 if n >= 8192 else 256
    src_tok, dst_row, tile_expert, tile_active = _dispatch_plan(idx[:, :TOP_K], n_experts, tm)
    n_tiles = src_tok.shape[0]
    fc = f // 4
    n_c = f // fc

    def wspec(shape, pick):
        return pl.BlockSpec(shape, lambda t, c, te, act: pick(te[t], jnp.where(act[t] == 1, c, n_c - 1)))

    smem_rows = lambda index_map: pl.BlockSpec((None, 1, tm), index_map, memory_space=pltpu.SMEM)
    grid_spec = pltpu.PrefetchScalarGridSpec(
        num_scalar_prefetch=2,
        grid=(n_tiles, n_c),
        in_specs=[smem_rows(lambda t, c, te, act: (t, 0, 0)),
                  smem_rows(lambda t, c, te, act: (jnp.minimum(t + 1, n_tiles - 1), 0, 0)),
                  smem_rows(lambda t, c, te, act: (t, 0, 0)),
                  pl.BlockSpec((1, d), lambda t, c, te, act: (0, 0)),
                  pl.BlockSpec(memory_space=pl.ANY),
                  wspec((None, None, d, fc), lambda e, c: (j, e, 0, c)),
                  wspec((None, None, d, fc), lambda e, c: (j, e, 0, c)),
                  wspec((None, None, fc, d), lambda e, c: (j, e, c, 0))],
        out_specs=pl.BlockSpec(memory_space=pl.ANY),
        scratch_shapes=[pltpu.VMEM((tm, d), F32), pltpu.VMEM((tm, d), BF16), pltpu.VMEM((tm, d), F32),
                        pltpu.SemaphoreType.DMA, pltpu.SemaphoreType.DMA],
    )
    y = pl.pallas_call(
        functools.partial(_moe_kernel, tm=tm),
        grid_spec=grid_spec,
        out_shape=jax.ShapeDtypeStruct((n_tiles * tm, d), F32),
        compiler_params=_params("arbitrary", "arbitrary"),
        name="ffn_moe",
    )(tile_expert, tile_active, src_tok, src_tok, dst_row, g.reshape(1, d), h, wg_b, wu_b, wd_b)
    return y, gates


def _ple_kernel(*refs, final, routed):
    refs = list(refs)
    h_ref = refs.pop(0)
    x = h_ref[...]
    if routed:
        y0_ref, y1_ref, gate_ref = refs.pop(0), refs.pop(0), refs.pop(0)
        gates = gate_ref[...]
        x = x + gates[:, 0:1] * y0_ref[...] + gates[:, 1:2] * y1_ref[...]
    p_ref, g_ref, wg_ref, wp_ref = refs[:4]
    out_ref = refs[-1]
    gate = jax.nn.sigmoid(_dot(_rms(x, g_ref[...]).astype(BF16), wg_ref[...]))
    h3 = x + gate * _dot(p_ref[...].astype(BF16), wp_ref[...])
    out_ref[...] = _rms(h3, refs[4][...]) if final else h3


def _ple(h, routed, p3, layer, g, gate_w_b, proj_w_b, final_g):
    n, d = h.shape
    dp = p3.shape[-1]
    tm = _tile(n, 512)
    final = final_g is not None
    row = pl.BlockSpec((tm, d), lambda r: (r, 0))
    vec = _resident((1, d), lambda r: (0, 0))
    args, in_specs = [h], [row]
    if routed is not None:
        y, gates = routed
        per = n // tm
        args += [y, y, gates]
        in_specs += [row, pl.BlockSpec((tm, d), lambda r: (per + r, 0)),
                     pl.BlockSpec((tm, ROUTER_LANES), lambda r: (r, 0))]
    args += [p3, g.reshape(1, d), gate_w_b, proj_w_b]
    in_specs += [pl.BlockSpec((None, tm, dp), lambda r: (layer, r, 0)), vec,
                 _resident((None, d, d), lambda r: (layer, 0, 0)),
                 _resident((None, dp, d), lambda r: (layer, 0, 0))]
    if final:
        args.append(final_g.reshape(1, d))
        in_specs.append(vec)
    return pl.pallas_call(
        functools.partial(_ple_kernel, final=final, routed=routed is not None),
        grid=(n // tm,),
        in_specs=in_specs,
        out_specs=row,
        out_shape=jax.ShapeDtypeStruct((n, d), F32),
        compiler_params=_params("parallel"),
        name="ple",
    )(*args)


def _trunk(x, p, cache_k, cache_v, state_conv, wts):
    batch, seq, d = x.shape
    depth = wts["w_in"].shape[0]
    aw = d // 2
    cw = d - aw
    n = batch * seq
    h = x.reshape(n, d)
    p3 = p.reshape(depth, n, p.shape[-1])
    prompt = cache_k is None
    if not prompt:
        cache_k4 = cache_k.reshape(cache_k.shape[:3] + (aw,))
        cache_v4 = cache_v.reshape(cache_v.shape[:3] + (aw,))
    ks, vs, cs = [], [], []
    for i in range(depth):
        lam_init = 0.8 - 0.6 * math.exp(-0.3 * i)
        lam_vecs = jnp.stack([wts["lam_q1"][i], wts["lam_k1"][i], wts["lam_q2"][i], wts["lam_k2"][i]])
        q, kf, vf, kb, vb, u = _in_proj(h, wts["norm_mix_g"][i], wts["w_in"], i, aw, cw, seq, transpose_v=prompt)
        if prompt:
            o = _attn_prompt(q, kb, vb, lam_vecs, wts["subln_g"][i], lam_init, batch, seq)
            hist = None
        else:
            o = _attn_sample(q, kb, vb, cache_k4, cache_v4, i, lam_vecs, wts["subln_g"][i], lam_init, batch, seq)
            hist = state_conv
        c = _conv_branch(u, hist, i, wts["dw_w"][i], wts["dw_b"][i], wts["conv_ln_g"][i], wts["conv_ln_b"][i],
                         batch, seq)
        h = _out_proj(h, o, c, wts["w_o"], i)
        j = i // 2
        if i % 2 == 0:
            h = _ffn_dense(h, wts["norm_ffn_g"][i], wts["ff_wg"], wts["ff_wu"], wts["ff_wd"], j)
            routed = None
        else:
            routed = _ffn_moe(h, wts["norm_ffn_g"][i], wts["router_w"], wts["moe_wg"], wts["moe_wu"],
                              wts["moe_wd"], j)
        h = _ple(h, routed, p3, i, wts["ple_norm_g"][i], wts["ple_gate_w"], wts["ple_proj_w"],
                 wts["final_norm_g"] if i == depth - 1 else None)
        ks.append(kf.reshape(batch, seq, N_HEADS, aw // N_HEADS))
        vs.append(vf.reshape(batch, seq, N_HEADS, aw // N_HEADS))
        cs.append(u.reshape(batch, seq, cw)[:, seq - CONV_BUF:])
    return h.reshape(batch, seq, d), jnp.stack(ks), jnp.stack(vs), jnp.stack(cs)


def kernel(x_prompt, x_sample, cache_k, cache_v, state_conv, p_prompt, p_sample, norm_mix_g, w_in, lam_q1, lam_k1, lam_q2, lam_k2, subln_g, dw_w, dw_b, conv_ln_g, conv_ln_b, w_o, norm_ffn_g, ff_wg, ff_wu, ff_wd, router_w, moe_wg, moe_wu, moe_wd, ple_norm_g, ple_gate_w, ple_proj_w, final_norm_g):
    n_experts = router_w.shape[-1]
    wts = dict(
        norm_mix_g=norm_mix_g, lam_q1=lam_q1, lam_k1=lam_k1, lam_q2=lam_q2, lam_k2=lam_k2, subln_g=subln_g,
        dw_w=dw_w, dw_b=dw_b, conv_ln_g=conv_ln_g, conv_ln_b=conv_ln_b, norm_ffn_g=norm_ffn_g,
        ple_norm_g=ple_norm_g, final_norm_g=final_norm_g,
        w_in=w_in.astype(BF16), w_o=w_o.astype(BF16),
        ff_wg=ff_wg.astype(BF16), ff_wu=ff_wu.astype(BF16), ff_wd=ff_wd.astype(BF16),
        router_w=jnp.pad(router_w, ((0, 0), (0, 0), (0, ROUTER_LANES - n_experts))).astype(BF16),
        moe_wg=moe_wg.astype(BF16), moe_wu=moe_wu.astype(BF16), moe_wd=moe_wd.astype(BF16),
        ple_gate_w=ple_gate_w.astype(BF16), ple_proj_w=ple_proj_w.astype(BF16),
    )
    y_p, k_p, v_p, c_p = _trunk(x_prompt, p_prompt, None, None, None, wts)
    y_s, k_s, v_s, c_s = _trunk(x_sample, p_sample, cache_k, cache_v, state_conv, wts)
    return (y_p, y_s, k_p, v_p, c_p, k_s, v_s, c_s)
```

```python
import functools
import math

import jax
import jax.numpy as jnp
from jax import lax
from jax.experimental import pallas as pl
from jax.experimental.pallas import tpu as pltpu

F32 = jnp.float32
BF16 = jnp.bfloat16

EPS = 1e-6
LOG2E = math.log2(math.e)
CHUNK = 64
N_HEADS = 4
DW_WIDTH = 31
CONV_BUF = DW_WIDTH - 1
HALO = 32
SUBLANES = 8
LANES = 128
V7X_MXU_DIM = 256
V7X_VMEM_LIMIT = 56 * 1024 * 1024


def _params(*sem):
    return pltpu.CompilerParams(dimension_semantics=sem, vmem_limit_bytes=V7X_VMEM_LIMIT)


def _tile(n, pref):
    if n <= pref:
        return n
    t = pref
    while n % t or t % 8:
        t -= 1
    return t


def _rms(x, g):
    return x * lax.rsqrt(jnp.mean(x * x, axis=-1, keepdims=True) + EPS) * g


def _dot(a, b):
    return jnp.dot(a, b, preferred_element_type=F32)


def _dot_nt(a, b):
    return lax.dot_general(a, b, (((1,), (1,)), ((), ())), preferred_element_type=F32)


def _resident(shape, index_map):
    return pl.BlockSpec(shape, index_map, pipeline_mode=pl.Buffered(1))


def _in_proj_kernel(*refs, aw, cw, q_scale, transpose_v):
    h_ref, g_ref, w_ref = refs[:3]
    q_ref, kf_ref, vf_ref, kb_ref, vb_ref, u_ref = refs[-6:]
    xn = _rms(h_ref[...], g_ref[...]).astype(BF16)

    def mm(lo, width):
        return _dot(xn, w_ref[:, lo:lo + width])

    q_ref[...] = (mm(0, aw) * q_scale).astype(BF16)
    k = mm(aw, aw)
    _to_token_tiles(kf_ref, k)
    kb_ref[...] = k.astype(BF16)
    v = mm(2 * aw, aw)
    _to_token_tiles(vf_ref, v)
    vb_ref[...] = (v.T if transpose_v else v).astype(BF16)
    ga = mm(3 * aw, cw)
    gb = mm(3 * aw + cw, cw)
    u_ref[...] = ga * jax.nn.sigmoid(gb)


def _in_proj(h, g, w_in_b, layer, kv_all, aw, cw, seq, transpose_v):
    n, d = h.shape
    depth = w_in_b.shape[0]
    tm = _tile(seq, 512)
    per = seq // tm
    hv = aw // N_HEADS
    q_scale = (hv // 2) ** -0.5 * LOG2E
    row = lambda width: pl.BlockSpec((tm, width), lambda r: (r, 0))
    if transpose_v:
        vb_spec = pl.BlockSpec((None, aw, tm), lambda r: (r // per, 0, r % per))
        vb_shape = jax.ShapeDtypeStruct((n // seq, aw, seq), BF16)
    else:
        vb_spec, vb_shape = row(aw), jax.ShapeDtypeStruct((n, aw), BF16)
    all_spec = pl.BlockSpec((None, tm * N_HEADS, hv), lambda r: (layer, r, 0))
    all_shape = jax.ShapeDtypeStruct((depth, n * N_HEADS, hv), F32)
    args = [h, g.reshape(1, d), w_in_b, *kv_all]
    in_specs = [row(d),
                _resident((1, d), lambda r: (0, 0)),
                _resident((None, d, w_in_b.shape[-1]), lambda r: (layer, 0, 0)),
                pl.BlockSpec(memory_space=pl.ANY), pl.BlockSpec(memory_space=pl.ANY)]
    aliases = {3: 1, 4: 2}
    return pl.pallas_call(
        functools.partial(_in_proj_kernel, aw=aw, cw=cw, q_scale=q_scale, transpose_v=transpose_v),
        grid=(n // tm,),
        in_specs=in_specs,
        out_specs=[row(aw), all_spec, all_spec, row(aw), vb_spec, row(cw)],
        out_shape=[jax.ShapeDtypeStruct((n, aw), BF16), all_shape, all_shape,
                   jax.ShapeDtypeStruct((n, aw), BF16), vb_shape,
                   jax.ShapeDtypeStruct((n, cw), F32)],
        input_output_aliases=aliases,
        compiler_params=_params("parallel"),
        name="in_proj",
    )(*args)


def _split_halves(q):
    dk = q.shape[-1] // 2
    lane = lax.broadcasted_iota(jnp.int32, q.shape, 1)
    zero = jnp.zeros_like(q)
    return jnp.concatenate([jnp.where(lane < dk, q, zero), jnp.where(lane >= dk, q, zero)], axis=0)


def _lam(lam_ref, lam_init):
    lv = lam_ref[...]
    a1 = jnp.sum(lv[0:1] * lv[1:2], axis=-1, keepdims=True)
    a2 = jnp.sum(lv[2:3] * lv[3:4], axis=-1, keepdims=True)
    return jnp.exp(a1) - jnp.exp(a2) + lam_init


ATTN_KC = V7X_MXU_DIM
ATTN_HEADS = 2


def _attn_prompt_kernel(lam_ref, q_ref, k_ref, vt_ref, g_ref, o_ref, qt_scr, sa_scr, sb_scr, m_scr, l_scr, acc_scr, *,
                        nh, lam_init):
    kc = ATTN_KC
    qb = kc
    qi = pl.program_id(2)
    hv = q_ref.shape[-1] // nh
    heads = range(nh)
    for h in heads:
        qq = _split_halves(q_ref[:, h * hv:(h + 1) * hv].astype(F32))
        qt_scr[h] = qq.T.astype(BF16)
    m_scr[...] = jnp.full(m_scr.shape, -jnp.inf, F32)
    l_scr[...] = jnp.zeros(l_scr.shape, F32)
    acc_scr[...] = jnp.zeros(acc_scr.shape, F32)

    def scores(c, s_ref):
        off = pl.multiple_of(c * kc, kc)
        for h in heads:
            s_ref[h] = _dot(k_ref[pl.ds(off, kc), h * hv:(h + 1) * hv], qt_scr[h])

    def consume(c, s_ref, masked):
        off = pl.multiple_of(c * kc, kc)
        for h in heads:
            s = s_ref[h]
            if masked:
                key = lax.broadcasted_iota(jnp.int32, s.shape, 0)
                col = lax.broadcasted_iota(jnp.int32, s.shape, 1)
                s = jnp.where(key // CHUNK <= (col % qb) // CHUNK, s, -jnp.inf)
            m_prev = m_scr[h]
            m_new = jnp.maximum(m_prev, jnp.max(s, axis=0, keepdims=True))
            alpha = jnp.exp2(m_prev - m_new)
            p = jnp.exp2(s - m_new)
            l_scr[h] = alpha * l_scr[h] + jnp.sum(p, axis=0, keepdims=True)
            acc_scr[h] = alpha * acc_scr[h] + _dot(vt_ref[h * hv:(h + 1) * hv, pl.ds(off, kc)], p.astype(BF16))
            m_scr[h] = m_new

    scores(0, sa_scr)

    def pair(t, carry):
        c = 2 * t
        scores(c + 1, sb_scr)
        consume(c, sa_scr, False)
        scores(c + 2, sa_scr)
        consume(c + 1, sb_scr, False)
        return carry

    lax.fori_loop(0, qi // 2, pair, 0)
    c0 = 2 * (qi // 2)

    @pl.when(qi % 2 == 1)
    def _():
        scores(c0 + 1, sb_scr)
        consume(c0, sa_scr, False)
        consume(c0 + 1, sb_scr, True)

    @pl.when(qi % 2 == 0)
    def _():
        consume(c0, sa_scr, True)

    lam = _lam(lam_ref, lam_init)
    for h in heads:
        oo = acc_scr[h] * (1.0 / l_scr[h])
        o = (oo[:, :qb] - lam * oo[:, qb:]).T
        o_ref[:, h * hv:(h + 1) * hv] = (_rms(o, g_ref[...]) * (1.0 - lam_init)).astype(BF16)


def _attn_prompt(q, k, vt3, lam_vecs, subln_g, lam_init, batch, seq):
    aw = q.shape[-1]
    hv = aw // N_HEADS
    nh = ATTN_HEADS
    qb = kc = ATTN_KC
    assert seq % qb == 0 and qb % CHUNK == 0 and N_HEADS % nh == 0
    q3, k3 = (a.reshape(batch, seq, aw) for a in (q, k))
    qspec = pl.BlockSpec((None, qb, nh * hv), lambda b, h, i: (b, i, h))
    out = pl.pallas_call(
        functools.partial(_attn_prompt_kernel, nh=nh, lam_init=lam_init),
        grid=(batch, N_HEADS // nh, seq // qb),
        in_specs=[_resident(lam_vecs.shape, lambda b, h, i: (0, 0)),
                  qspec,
                  pl.BlockSpec((None, seq, nh * hv), lambda b, h, i: (b, 0, h)),
                  pl.BlockSpec((None, nh * hv, seq), lambda b, h, i: (b, h, 0)),
                  _resident((1, hv), lambda b, h, i: (0, 0))],
        out_specs=qspec,
        out_shape=jax.ShapeDtypeStruct((batch, seq, aw), BF16),
        scratch_shapes=[pltpu.VMEM((nh, hv, 2 * qb), BF16),
                        pltpu.VMEM((nh, kc, 2 * qb), F32), pltpu.VMEM((nh, kc, 2 * qb), F32),
                        pltpu.VMEM((nh, 1, 2 * qb), F32), pltpu.VMEM((nh, 1, 2 * qb), F32),
                        pltpu.VMEM((nh, hv, 2 * qb), F32)],
        compiler_params=_params("parallel", "parallel", "arbitrary"),
        name="attn_prompt",
    )(lam_vecs, q3, k3, vt3, subln_g.reshape(1, hv))
    return out.reshape(batch * seq, aw)


def _attn_sample_kernel(lam_ref, q_ref, kc_ref, vc_ref, kn_ref, vn_ref, g_ref, o_ref, *, lam_init):
    t = q_ref.shape[0]
    hv = q_ref.shape[-1] // N_HEADS
    past = kc_ref.shape[0] // N_HEADS
    lam = _lam(lam_ref, lam_init)
    for h in range(N_HEADS):
        cols = slice(h * hv, (h + 1) * hv)
        qq = _split_halves(q_ref[:, cols])
        kc = kc_ref[pl.ds(h, past, stride=N_HEADS), :].astype(BF16)
        vc = vc_ref[pl.ds(h, past, stride=N_HEADS), :].astype(BF16)
        s_c = _dot_nt(qq, kc)
        s_n = _dot_nt(qq, kn_ref[:, cols])
        m = jnp.maximum(jnp.max(s_c, axis=-1, keepdims=True), jnp.max(s_n, axis=-1, keepdims=True))
        e_c = jnp.exp2(s_c - m)
        e_n = jnp.exp2(s_n - m)
        l = jnp.sum(e_c, axis=-1, keepdims=True) + jnp.sum(e_n, axis=-1, keepdims=True)
        acc = _dot(e_c.astype(BF16), vc) + _dot(e_n.astype(BF16), vn_ref[:, cols])
        oo = acc * (1.0 / l)
        o = oo[:t] - lam * oo[t:]
        o_ref[:, cols] = (_rms(o, g_ref[...]) * (1.0 - lam_init)).astype(BF16)


def _attn_sample(q, k, v, cache_k4, cache_v4, layer, lam_vecs, subln_g, lam_init, batch, seq):
    aw = q.shape[-1]
    hv = aw // N_HEADS
    q3, k3, v3 = (a.reshape(batch, seq, aw) for a in (q, k, v))
    new = pl.BlockSpec((None, seq, aw), lambda b: (b, 0, 0))
    cache = pl.BlockSpec((None, None, cache_k4.shape[2], hv), lambda b: (layer, b, 0, 0))
    out = pl.pallas_call(
        functools.partial(_attn_sample_kernel, lam_init=lam_init),
        grid=(batch,),
        in_specs=[_resident(lam_vecs.shape, lambda b: (0, 0)),
                  new, cache, cache, new, new,
                  _resident((1, hv), lambda b: (0, 0))],
        out_specs=new,
        out_shape=jax.ShapeDtypeStruct((batch, seq, aw), BF16),
        compiler_params=_params("parallel"),
        name="attn_sample",
    )(lam_vecs, q3, cache_k4, cache_v4, k3, v3, subln_g.reshape(1, hv))
    return out.reshape(batch * seq, aw)


CONV_ROWS = 32


def _conv_kernel(*refs, tt, has_hist, multi_tile):
    refs = list(refs)
    u_ref = refs.pop(0)
    prev_ref = refs.pop(0) if multi_tile else None
    hist_ref = refs.pop(0) if has_hist else None
    w_ref, b_ref, lg_ref, lb_ref, c_ref, state_ref, ext = refs
    cw = u_ref.shape[-1]
    t = pl.program_id(1)

    def first_tile_halo():
        ext[0:HALO, :] = jnp.zeros((HALO, cw), F32)
        if has_hist:
            ext[HALO - CONV_BUF:HALO, :] = hist_ref[...]

    if multi_tile:
        pl.when(t == 0)(first_tile_halo)

        @pl.when(t > 0)
        def _():
            ext[0:HALO, :] = prev_ref[...]
    else:
        first_tile_halo()
    ext[HALO:HALO + tt, :] = u_ref[...]

    bias = b_ref[...]
    lg = lg_ref[...]
    lb = lb_ref[...]
    base = HALO - CONV_BUF

    def group(r, carry):
        r0 = pl.multiple_of(r * CONV_ROWS, CONV_ROWS)
        win = ext[pl.ds(r0, HALO + CONV_ROWS), :]
        acc = jnp.broadcast_to(bias, (CONV_ROWS, cw))
        nwin = HALO + CONV_ROWS
        for b in range(SUBLANES):
            shifted = pltpu.roll(win, (nwin - base - b) % nwin, axis=0)
            for j in range(b, DW_WIDTH, SUBLANES):
                acc = acc + w_ref[j:j + 1, :] * shifted[j - b:j - b + CONV_ROWS, :]
        mu = jnp.mean(acc, axis=-1, keepdims=True)
        xc = acc - mu
        var = jnp.mean(xc * xc, axis=-1, keepdims=True)
        y = xc * lax.rsqrt(var + EPS) * lg + lb
        c_ref[pl.ds(r0, CONV_ROWS), :] = (y * jax.nn.sigmoid(y)).astype(BF16)
        return carry

    lax.fori_loop(0, tt // CONV_ROWS, group, 0)

    @pl.when(t == pl.num_programs(1) - 1)
    def _():
        state_ref[...] = ext[HALO + tt - CONV_BUF:HALO + tt, :]


def _conv_branch(u, hist4, layer, dw_w, dw_b, ln_g, ln_b, batch, seq):
    cw = u.shape[-1]
    tt = _tile(seq, 512)
    nt = seq // tt
    multi_tile = nt > 1
    has_hist = hist4 is not None
    u3 = u.reshape(batch, seq, cw)
    args = [u3]
    in_specs = [pl.BlockSpec((None, tt, cw), lambda b, t: (b, t, 0))]
    if multi_tile:
        per = tt // HALO
        args.append(u3)
        in_specs.append(pl.BlockSpec((None, HALO, cw), lambda b, t: (b, jnp.maximum(t * per - 1, 0), 0)))
    if has_hist:
        args.append(hist4)
        in_specs.append(pl.BlockSpec((None, None, CONV_BUF, cw), lambda b, t: (layer, b, 0, 0)))
    vec = _resident((1, cw), lambda b, t: (0, 0))
    args += [dw_w, dw_b.reshape(1, cw), ln_g.reshape(1, cw), ln_b.reshape(1, cw)]
    in_specs += [_resident((DW_WIDTH, cw), lambda b, t: (0, 0)), vec, vec, vec]
    out, state = pl.pallas_call(
        functools.partial(_conv_kernel, tt=tt, has_hist=has_hist, multi_tile=multi_tile),
        grid=(batch, nt),
        in_specs=in_specs,
        out_specs=[pl.BlockSpec((None, tt, cw), lambda b, t: (b, t, 0)),
                   pl.BlockSpec((None, CONV_BUF, cw), lambda b, t: (b, 0, 0))],
        out_shape=[jax.ShapeDtypeStruct((batch, seq, cw), BF16),
                   jax.ShapeDtypeStruct((batch, CONV_BUF, cw), F32)],
        scratch_shapes=[pltpu.VMEM((HALO + tt, cw), F32)],
        compiler_params=_params("parallel", "arbitrary"),
        name="conv_branch",
    )(*args)
    return out.reshape(batch * seq, cw), state


def _out_proj_kernel(h_ref, o_ref, c_ref, w_ref, out_ref, *, aw):
    out_ref[...] = h_ref[...] + _dot(o_ref[...], w_ref[0:aw, :]) + _dot(c_ref[...], w_ref[aw:, :])


def _out_proj(h, o, c, w_o_b, layer):
    n, d = h.shape
    aw = o.shape[-1]
    tm = _tile(n, 512)
    row = lambda width: pl.BlockSpec((tm, width), lambda r: (r, 0))
    return pl.pallas_call(
        functools.partial(_out_proj_kernel, aw=aw),
        grid=(n // tm,),
        in_specs=[row(d), row(aw), row(c.shape[-1]), _resident((None, d, d), lambda r: (layer, 0, 0))],
        out_specs=row(d),
        out_shape=jax.ShapeDtypeStruct((n, d), F32),
        compiler_params=_params("parallel"),
        name="out_proj",
    )(h, o, c, w_o_b)


def _swiglu_chunk(xn, wg, wu, wd):
    g = _dot(xn, wg)
    u = _dot(xn, wu)
    return _dot((g * jax.nn.sigmoid(g) * u).astype(BF16), wd)


def _ffn_dense_kernel(h_ref, g_ref, wg_ref, wu_ref, wd_ref, out_ref, *, fc):
    x = h_ref[...]
    xn = _rms(x, g_ref[...]).astype(BF16)
    acc = x
    for lo in range(0, wg_ref.shape[-1], fc):
        acc = acc + _swiglu_chunk(xn, wg_ref[:, lo:lo + fc], wu_ref[:, lo:lo + fc], wd_ref[lo:lo + fc, :])
    out_ref[...] = acc


def _ffn_dense(h, g, wg_b, wu_b, wd_b, j):
    n, d = h.shape
    f = wg_b.shape[-1]
    tm = _tile(n, 512)
    fc = f // 2
    row = pl.BlockSpec((tm, d), lambda r: (r, 0))
    return pl.pallas_call(
        functools.partial(_ffn_dense_kernel, fc=fc),
        grid=(n // tm,),
        in_specs=[row, _resident((1, d), lambda r: (0, 0)),
                  _resident((None, d, f), lambda r: (j, 0, 0)),
                  _resident((None, d, f), lambda r: (j, 0, 0)),
                  _resident((None, f, d), lambda r: (j, 0, 0))],
        out_specs=row,
        out_shape=jax.ShapeDtypeStruct((n, d), F32),
        compiler_params=_params("parallel"),
        name="ffn_dense",
    )(h, g.reshape(1, d), wg_b, wu_b, wd_b)


ROUTER_LANES = 128
TOP_K = 2


def _to_token_tiles(ref, x):
    rows, d = x.shape
    per = d // LANES
    for s in range(per):
        ref[pl.ds(s, rows, stride=per), :] = x[:, s * LANES:(s + 1) * LANES]


def _from_token_tiles(ref, rows):
    per = ref.shape[0] // rows
    return jnp.concatenate([ref[pl.ds(s, rows, stride=per), :] for s in range(per)], axis=-1)


def _router_kernel(h_ref, g_ref, rw_ref, idx_ref, gate_ref, xt_ref, *, n_experts):
    xn = _rms(h_ref[...], g_ref[...])
    _to_token_tiles(xt_ref, xn)
    logits = _dot(xn.astype(BF16), rw_ref[...])
    lane = lax.broadcasted_iota(jnp.int32, logits.shape, 1)
    neg = jnp.float32(-jnp.inf)
    lg = jnp.where(lane < n_experts, logits, neg)
    m1 = jnp.max(lg, axis=-1, keepdims=True)
    i1 = jnp.min(jnp.where(lg == m1, lane, ROUTER_LANES), axis=-1, keepdims=True)
    lg2 = jnp.where(lane == i1, neg, lg)
    m2 = jnp.max(lg2, axis=-1, keepdims=True)
    i2 = jnp.min(jnp.where(lg2 == m2, lane, ROUTER_LANES), axis=-1, keepdims=True)
    e2 = jnp.exp(m2 - m1)
    den = 1.0 + e2
    idx_ref[...] = jnp.where(lane == 0, i1, jnp.where(lane == 1, i2, 0))
    gate_ref[...] = jnp.where(lane == 0, 1.0 / den, jnp.where(lane == 1, e2 / den, 0.0))


def _router(h, g, router_b, j, n_experts):
    n, d = h.shape
    tm = _tile(n, 512)
    lanes = pl.BlockSpec((tm, ROUTER_LANES), lambda r: (r, 0))
    per = d // LANES
    return pl.pallas_call(
        functools.partial(_router_kernel, n_experts=n_experts),
        grid=(n // tm,),
        in_specs=[pl.BlockSpec((tm, d), lambda r: (r, 0)), _resident((1, d), lambda r: (0, 0)),
                  _resident((None, d, ROUTER_LANES), lambda r: (j, 0, 0))],
        out_specs=[lanes, lanes, pl.BlockSpec((tm * per, LANES), lambda r: (r, 0))],
        out_shape=[jax.ShapeDtypeStruct((n, ROUTER_LANES), jnp.int32),
                   jax.ShapeDtypeStruct((n, ROUTER_LANES), F32),
                   jax.ShapeDtypeStruct((n * per, LANES), F32)],
        compiler_params=_params("parallel"),
        name="router",
    )(h, g.reshape(1, d), router_b)


def _dispatch_plan(top_i, n_experts, tm):
    n = top_i.shape[0]
    n_slots = TOP_K * n
    n_tiles = -(-n_slots // tm) + n_experts
    n_pos = n_tiles * tm
    e_flat = top_i.reshape(-1)
    order = jnp.argsort(e_flat, stable=True).astype(jnp.int32)
    counts = jnp.sum((e_flat[:, None] == jnp.arange(n_experts, dtype=jnp.int32)[None, :]).astype(jnp.int32), axis=0)
    padded = -(-counts // tm) * tm
    ends = jnp.cumsum(padded)
    off = ends - padded
    coff = jnp.cumsum(counts) - counts
    pos = jnp.arange(n_pos, dtype=jnp.int32)
    e_of = jnp.minimum(jnp.searchsorted(ends, pos, side="right").astype(jnp.int32), n_experts - 1)
    r = pos - off[e_of]
    valid = (pos < ends[-1]) & (r < counts[e_of])
    slot = order[jnp.clip(coff[e_of] + r, 0, n_slots - 1)]
    tok = slot // TOP_K
    pad_rank = jnp.cumsum((~valid).astype(jnp.int32)) - 1
    src_tok = jnp.where(valid, tok, 0)
    dst_row = jnp.where(valid, (slot % TOP_K) * n + tok, n_slots + pad_rank)
    tile_pos = jnp.arange(n_tiles, dtype=jnp.int32) * tm
    tile_expert = e_of[tile_pos]
    tile_active = (tile_pos < ends[-1]).astype(jnp.int32)
    return src_tok.reshape(n_tiles, 1, tm), dst_row.reshape(n_tiles, 1, tm), tile_expert, tile_active


def _moe_kernel(te_ref, act_ref, src_ref, nsrc_ref, dst_ref, xt_hbm, wg_ref, wu_ref, wd_ref, y_hbm,
                xbuf, xn_scr, acc, ybuf, gsem, ssem, *, tm):
    t = pl.program_id(0)
    c = pl.program_id(1)
    n_t = pl.num_programs(0)
    n_c = pl.num_programs(1)
    per = xbuf.shape[0] // tm
    active = act_ref[t] == 1
    prev_active = (t > 0) & (act_ref[jnp.maximum(t - 1, 0)] == 1)
    next_active = (t + 1 < n_t) & (act_ref[jnp.minimum(t + 1, n_t - 1)] == 1)

    def tile_at(ref, row):
        return ref.at[pl.ds(pl.multiple_of(row * per, per), per)]

    def gather(idx_ref):
        def issue(r, carry):
            pltpu.make_async_copy(tile_at(xt_hbm, idx_ref[0, r]), tile_at(xbuf, r), gsem).start()
            return carry
        lax.fori_loop(0, tm, issue, 0, unroll=8)

    def wait_gather():
        def one(r, carry):
            pltpu.make_async_copy(tile_at(xt_hbm, 0), tile_at(xbuf, r), gsem).wait()
            return carry
        lax.fori_loop(0, tm, one, 0, unroll=8)

    def scatter():
        def issue(r, carry):
            pltpu.make_async_copy(tile_at(ybuf, r), tile_at(y_hbm, dst_ref[0, r]), ssem).start()
            return carry
        lax.fori_loop(0, tm, issue, 0, unroll=8)

    def wait_scatter():
        def one(r, carry):
            pltpu.make_async_copy(tile_at(ybuf, r), tile_at(y_hbm, 0), ssem).wait()
            return carry
        lax.fori_loop(0, tm, one, 0, unroll=8)

    @pl.when((c == 0) & (t == 0) & active)
    def _():
        gather(src_ref)

    @pl.when((c == 0) & active)
    def _():
        wait_gather()
        xn_scr[...] = _from_token_tiles(xbuf, tm).astype(BF16)

        pl.when(next_active)(lambda: gather(nsrc_ref))

    @pl.when(active)
    def _():
        y = _swiglu_chunk(xn_scr[...], wg_ref[...], wu_ref[...], wd_ref[...])

        @pl.when(c == 0)
        def _():
            acc[...] = y

        @pl.when((c > 0) & (c < n_c - 1))
        def _():
            acc[...] += y

        @pl.when(c == n_c - 1)
        def _():
            pl.when(prev_active)(wait_scatter)
            _to_token_tiles(ybuf, acc[...] + y)
            scatter()
            pl.when(t == n_t - 1)(wait_scatter)

    @pl.when((c == n_c - 1) & jnp.logical_not(active) & prev_active)
    def _():
        wait_scatter()


def _ffn_moe(h, g, router_b, wg_b, wu_b, wd_b, j):
    n, d = h.shape
    n_experts, f = wg_b.shape[1], wg_b.shape[-1]
    idx, gates, xt = _router(h, g, router_b, j, n_experts)
    tm = 1024 if n >= 8192 else 256
    src_tok, dst_row, tile_expert, tile_active = _dispatch_plan(idx[:, :TOP_K], n_experts, tm)
    n_tiles = src_tok.shape[0]
    per = d // LANES
    fc = 2 * V7X_MXU_DIM
    n_c = f // fc
    assert f % fc == 0 and n_c >= 2

    def wspec(shape, pick):
        return pl.BlockSpec(shape, lambda t, c, te, act: pick(te[t], jnp.where(act[t] == 1, c, n_c - 1)))

    smem_rows = lambda index_map: pl.BlockSpec((None, 1, tm), index_map, memory_space=pltpu.SMEM)
    grid_spec = pltpu.PrefetchScalarGridSpec(
        num_scalar_prefetch=2,
        grid=(n_tiles, n_c),
        in_specs=[smem_rows(lambda t, c, te, act: (t, 0, 0)),
                  smem_rows(lambda t, c, te, act: (jnp.minimum(t + 1, n_tiles - 1), 0, 0)),
                  smem_rows(lambda t, c, te, act: (t, 0, 0)),
                  pl.BlockSpec(memory_space=pl.ANY),
                  wspec((None, None, d, fc), lambda e, c: (j, e, 0, c)),
                  wspec((None, None, d, fc), lambda e, c: (j, e, 0, c)),
                  wspec((None, None, fc, d), lambda e, c: (j, e, c, 0))],
        out_specs=pl.BlockSpec(memory_space=pl.ANY),
        scratch_shapes=[pltpu.VMEM((tm * per, LANES), F32), pltpu.VMEM((tm, d), BF16), pltpu.VMEM((tm, d), F32),
                        pltpu.VMEM((tm * per, LANES), F32),
                        pltpu.SemaphoreType.DMA, pltpu.SemaphoreType.DMA],
    )
    y = pl.pallas_call(
        functools.partial(_moe_kernel, tm=tm),
        grid_spec=grid_spec,
        out_shape=jax.ShapeDtypeStruct((n_tiles * tm * per, LANES), F32),
        compiler_params=_params("arbitrary", "arbitrary"),
        name="ffn_moe",
    )(tile_expert, tile_active, src_tok, src_tok, dst_row, xt, wg_b, wu_b, wd_b)
    return y, gates


def _ple_kernel(*refs, final, routed):
    refs = list(refs)
    h_ref = refs.pop(0)
    x = h_ref[...]
    if routed:
        y0_ref, y1_ref, gate_ref = refs.pop(0), refs.pop(0), refs.pop(0)
        gates = gate_ref[...]
        rows = x.shape[0]
        x = x + gates[:, 0:1] * _from_token_tiles(y0_ref, rows) + gates[:, 1:2] * _from_token_tiles(y1_ref, rows)
    p_ref, g_ref, wg_ref, wp_ref = refs[:4]
    out_ref = refs[-1]
    gate = jax.nn.sigmoid(_dot(_rms(x, g_ref[...]).astype(BF16), wg_ref[...]))
    h3 = x + gate * _dot(p_ref[...].astype(BF16), wp_ref[...])
    out_ref[...] = _rms(h3, refs[4][...]) if final else h3


def _ple(h, routed, p3, layer, g, gate_w_b, proj_w_b, final_g):
    n, d = h.shape
    dp = p3.shape[-1]
    tm = _tile(n, 512)
    final = final_g is not None
    row = pl.BlockSpec((tm, d), lambda r: (r, 0))
    vec = _resident((1, d), lambda r: (0, 0))
    args, in_specs = [h], [row]
    if routed is not None:
        y, gates = routed
        steps = n // tm
        tiles = pl.BlockSpec((tm * (d // LANES), LANES), lambda r: (r, 0))
        second = pl.BlockSpec((tm * (d // LANES), LANES), lambda r: (steps + r, 0))
        args += [y, y, gates]
        in_specs += [tiles, second, pl.BlockSpec((tm, ROUTER_LANES), lambda r: (r, 0))]
    args += [p3, g.reshape(1, d), gate_w_b, proj_w_b]
    in_specs += [pl.BlockSpec((None, tm, dp), lambda r: (layer, r, 0)), vec,
                 _resident((None, d, d), lambda r: (layer, 0, 0)),
                 _resident((None, dp, d), lambda r: (layer, 0, 0))]
    if final:
        args.append(final_g.reshape(1, d))
        in_specs.append(vec)
    return pl.pallas_call(
        functools.partial(_ple_kernel, final=final, routed=routed is not None),
        grid=(n // tm,),
        in_specs=in_specs,
        out_specs=row,
        out_shape=jax.ShapeDtypeStruct((n, d), F32),
        compiler_params=_params("parallel"),
        name="ple",
    )(*args)


def _trunk(x, p, cache_k, cache_v, state_conv, wts):
    batch, seq, d = x.shape
    depth = wts["w_in"].shape[0]
    aw = d // 2
    cw = d - aw
    n = batch * seq
    h = x.reshape(n, d)
    p3 = p.reshape(depth, n, p.shape[-1])
    prompt = cache_k is None
    if not prompt:
        cache_k4 = cache_k.reshape(cache_k.shape[:2] + (-1, cache_k.shape[-1]))
        cache_v4 = cache_v.reshape(cache_v.shape[:2] + (-1, cache_v.shape[-1]))
    kv_all = tuple(jnp.zeros((depth, n * N_HEADS, aw // N_HEADS), F32) for _ in range(2))
    cs = []
    for i in range(depth):
        lam_init = 0.8 - 0.6 * math.exp(-0.3 * i)
        lam_vecs = jnp.stack([wts["lam_q1"][i], wts["lam_k1"][i], wts["lam_q2"][i], wts["lam_k2"][i]])
        q, k_all, v_all, kb, vb, u = _in_proj(h, wts["norm_mix_g"][i], wts["w_in"], i, kv_all, aw, cw, seq,
                                              transpose_v=prompt)
        kv_all = (k_all, v_all)
        if prompt:
            o = _attn_prompt(q, kb, vb, lam_vecs, wts["subln_g"][i], lam_init, batch, seq)
            hist = None
        else:
            o = _attn_sample(q, kb, vb, cache_k4, cache_v4, i, lam_vecs, wts["subln_g"][i], lam_init, batch, seq)
            hist = state_conv
        c, conv_state = _conv_branch(u, hist, i, wts["dw_w"][i], wts["dw_b"][i], wts["conv_ln_g"][i],
                                     wts["conv_ln_b"][i], batch, seq)
        h = _out_proj(h, o, c, wts["w_o"], i)
        j = i // 2
        if i % 2 == 0:
            h = _ffn_dense(h, wts["norm_ffn_g"][i], wts["ff_wg"], wts["ff_wu"], wts["ff_wd"], j)
            routed = None
        else:
            routed = _ffn_moe(h, wts["norm_ffn_g"][i], wts["router_w"], wts["moe_wg"], wts["moe_wu"],
                              wts["moe_wd"], j)
        h = _ple(h, routed, p3, i, wts["ple_norm_g"][i], wts["ple_gate_w"], wts["ple_proj_w"],
                 wts["final_norm_g"] if i == depth - 1 else None)
        cs.append(conv_state)
    kv_shape = (depth, batch, seq, N_HEADS, aw // N_HEADS)
    return h.reshape(batch, seq, d), kv_all[0].reshape(kv_shape), kv_all[1].reshape(kv_shape), jnp.stack(cs)


def kernel(x_prompt, x_sample, cache_k, cache_v, state_conv, p_prompt, p_sample, norm_mix_g, w_in, lam_q1, lam_k1, lam_q2, lam_k2, subln_g, dw_w, dw_b, conv_ln_g, conv_ln_b, w_o, norm_ffn_g, ff_wg, ff_wu, ff_wd, router_w, moe_wg, moe_wu, moe_wd, ple_norm_g, ple_gate_w, ple_proj_w, final_norm_g):
    n_experts = router_w.shape[-1]
    wts = dict(
        norm_mix_g=norm_mix_g, lam_q1=lam_q1, lam_k1=lam_k1, lam_q2=lam_q2, lam_k2=lam_k2, subln_g=subln_g,
        dw_w=dw_w, dw_b=dw_b, conv_ln_g=conv_ln_g, conv_ln_b=conv_ln_b, norm_ffn_g=norm_ffn_g,
        ple_norm_g=ple_norm_g, final_norm_g=final_norm_g,
        w_in=w_in.astype(BF16), w_o=w_o.astype(BF16),
        ff_wg=ff_wg.astype(BF16), ff_wu=ff_wu.astype(BF16), ff_wd=ff_wd.astype(BF16),
        router_w=jnp.pad(router_w, ((0, 0), (0, 0), (0, ROUTER_LANES - n_experts))).astype(BF16),
        moe_wg=moe_wg.astype(BF16), moe_wu=moe_wu.astype(BF16), moe_wd=moe_wd.astype(BF16),
        ple_gate_w=ple_gate_w.astype(BF16), ple_proj_w=ple_proj_w.astype(BF16),
    )
    y_p, k_p, v_p, c_p = _trunk(x_prompt, p_prompt, None, None, None, wts)
    y_s, k_s, v_s, c_s = _trunk(x_sample, p_sample, cache_k, cache_v, state_conv, wts)
    return (y_p, y_s, k_p, v_p, c_p, k_s, v_s, c_s)
```

```python
import functools
import math

import jax
import jax.numpy as jnp
from jax import lax
from jax.experimental import pallas as pl
from jax.experimental.pallas import tpu as pltpu

F32 = jnp.float32
BF16 = jnp.bfloat16

EPS = 1e-6
LOG2E = math.log2(math.e)
CHUNK = 64
N_HEADS = 4
DW_WIDTH = 31
CONV_BUF = DW_WIDTH - 1
HALO = 32
SUBLANES = 8
LANES = 128
V7X_MXU_DIM = 256
V7X_VMEM_LIMIT = 56 * 1024 * 1024


def _params(*sem):
    return pltpu.CompilerParams(dimension_semantics=sem, vmem_limit_bytes=V7X_VMEM_LIMIT)


def _tile(n, pref):
    if n <= pref:
        return n
    t = pref
    while n % t or t % 8:
        t -= 1
    return t


def _rms(x, g):
    return x * lax.rsqrt(jnp.mean(x * x, axis=-1, keepdims=True) + EPS) * g


def _dot(a, b):
    return jnp.dot(a, b, preferred_element_type=F32)


def _dot_nt(a, b):
    return lax.dot_general(a, b, (((1,), (1,)), ((), ())), preferred_element_type=F32)


def _resident(shape, index_map):
    return pl.BlockSpec(shape, index_map, pipeline_mode=pl.Buffered(1))


def _in_proj_kernel(*refs, aw, cw, q_scale, transpose_v):
    h_ref, g_ref, w_ref = refs[:3]
    q_ref, kf_ref, vf_ref, kb_ref, vb_ref, u_ref = refs[-6:]
    xn = _rms(h_ref[...], g_ref[...]).astype(BF16)

    def mm(lo, width):
        return _dot(xn, w_ref[:, lo:lo + width])

    q_ref[...] = (mm(0, aw) * q_scale).astype(BF16)
    k = mm(aw, aw)
    _to_token_tiles(kf_ref, k)
    kb_ref[...] = k.astype(BF16)
    v = mm(2 * aw, aw)
    _to_token_tiles(vf_ref, v)
    vb_ref[...] = (v.T if transpose_v else v).astype(BF16)
    ga = mm(3 * aw, cw)
    gb = mm(3 * aw + cw, cw)
    u_ref[...] = ga * jax.nn.sigmoid(gb)


def _in_proj(h, g, w_in_b, layer, kv_all, aw, cw, seq, transpose_v):
    n, d = h.shape
    depth = w_in_b.shape[0]
    tm = _tile(seq, 512)
    per = seq // tm
    hv = aw // N_HEADS
    q_scale = (hv // 2) ** -0.5 * LOG2E
    row = lambda width: pl.BlockSpec((tm, width), lambda r: (r, 0))
    if transpose_v:
        vb_spec = pl.BlockSpec((None, aw, tm), lambda r: (r // per, 0, r % per))
        vb_shape = jax.ShapeDtypeStruct((n // seq, aw, seq), BF16)
    else:
        vb_spec, vb_shape = row(aw), jax.ShapeDtypeStruct((n, aw), BF16)
    all_spec = pl.BlockSpec((None, tm * N_HEADS, hv), lambda r: (layer, r, 0))
    all_shape = jax.ShapeDtypeStruct((depth, n * N_HEADS, hv), F32)
    args = [h, g.reshape(1, d), w_in_b, *kv_all]
    in_specs = [row(d),
                _resident((1, d), lambda r: (0, 0)),
                _resident((None, d, w_in_b.shape[-1]), lambda r: (layer, 0, 0)),
                pl.BlockSpec(memory_space=pl.ANY), pl.BlockSpec(memory_space=pl.ANY)]
    aliases = {3: 1, 4: 2}
    return pl.pallas_call(
        functools.partial(_in_proj_kernel, aw=aw, cw=cw, q_scale=q_scale, transpose_v=transpose_v),
        grid=(n // tm,),
        in_specs=in_specs,
        out_specs=[row(aw), all_spec, all_spec, row(aw), vb_spec, row(cw)],
        out_shape=[jax.ShapeDtypeStruct((n, aw), BF16), all_shape, all_shape,
                   jax.ShapeDtypeStruct((n, aw), BF16), vb_shape,
                   jax.ShapeDtypeStruct((n, cw), F32)],
        input_output_aliases=aliases,
        compiler_params=_params("parallel"),
        name="in_proj",
    )(*args)


def _split_halves(q):
    dk = q.shape[-1] // 2
    lane = lax.broadcasted_iota(jnp.int32, q.shape, 1)
    zero = jnp.zeros_like(q)
    return jnp.concatenate([jnp.where(lane < dk, q, zero), jnp.where(lane >= dk, q, zero)], axis=0)


def _lam(lam_ref, lam_init):
    lv = lam_ref[...]
    a1 = jnp.sum(lv[0:1] * lv[1:2], axis=-1, keepdims=True)
    a2 = jnp.sum(lv[2:3] * lv[3:4], axis=-1, keepdims=True)
    return jnp.exp(a1) - jnp.exp(a2) + lam_init


ATTN_KC = V7X_MXU_DIM
ATTN_HEADS = 2
ATTN_UNROLL = 4


def _attn_prompt_kernel(lam_ref, q_ref, k_ref, vt_ref, g_ref, o_ref, qt_scr, sa_scr, sb_scr, m_scr, l_scr, acc_scr, *,
                        nh, lam_init):
    kc = ATTN_KC
    qb = kc
    qi = pl.program_id(2)
    hv = q_ref.shape[-1] // nh
    heads = range(nh)
    for h in heads:
        qq = _split_halves(q_ref[:, h * hv:(h + 1) * hv].astype(F32))
        qt_scr[h] = qq.T.astype(BF16)
    m_scr[...] = jnp.full(m_scr.shape, -jnp.inf, F32)
    l_scr[...] = jnp.zeros(l_scr.shape, F32)
    acc_scr[...] = jnp.zeros(acc_scr.shape, F32)

    def scores(c, s_ref):
        off = pl.multiple_of(c * kc, kc)
        for h in heads:
            s_ref[h] = _dot(k_ref[pl.ds(off, kc), h * hv:(h + 1) * hv], qt_scr[h])

    def consume(c, s_ref, masked):
        off = pl.multiple_of(c * kc, kc)
        for h in heads:
            s = s_ref[h]
            if masked:
                key = lax.broadcasted_iota(jnp.int32, s.shape, 0)
                col = lax.broadcasted_iota(jnp.int32, s.shape, 1)
                s = jnp.where(key // CHUNK <= (col % qb) // CHUNK, s, -jnp.inf)
            m_prev = m_scr[h]
            m_new = jnp.maximum(m_prev, jnp.max(s, axis=0, keepdims=True))
            alpha = jnp.exp2(m_prev - m_new)
            p = jnp.exp2(s - m_new)
            l_scr[h] = alpha * l_scr[h] + jnp.sum(p, axis=0, keepdims=True)
            acc_scr[h] = alpha * acc_scr[h] + _dot(vt_ref[h * hv:(h + 1) * hv, pl.ds(off, kc)], p.astype(BF16))
            m_scr[h] = m_new

    scores(0, sa_scr)

    def pair(t, carry):
        c = 2 * t
        scores(c + 1, sb_scr)
        consume(c, sa_scr, False)
        scores(c + 2, sa_scr)
        consume(c + 1, sb_scr, False)
        return carry

    def unrolled(t, carry):
        for u in range(ATTN_UNROLL):
            pair(ATTN_UNROLL * t + u, carry)
        return carry

    n_long = qi // (2 * ATTN_UNROLL)
    lax.fori_loop(0, n_long, unrolled, 0)
    lax.fori_loop(ATTN_UNROLL * n_long, qi // 2, pair, 0)
    c0 = 2 * (qi // 2)

    @pl.when(qi % 2 == 1)
    def _():
        scores(c0 + 1, sb_scr)
        consume(c0, sa_scr, False)
        consume(c0 + 1, sb_scr, True)

    @pl.when(qi % 2 == 0)
    def _():
        consume(c0, sa_scr, True)

    lam = _lam(lam_ref, lam_init)
    for h in heads:
        oo = acc_scr[h] * (1.0 / l_scr[h])
        o = (oo[:, :qb] - lam * oo[:, qb:]).T
        o_ref[:, h * hv:(h + 1) * hv] = (_rms(o, g_ref[...]) * (1.0 - lam_init)).astype(BF16)


def _attn_prompt(q, k, vt3, lam_vecs, subln_g, lam_init, batch, seq):
    aw = q.shape[-1]
    hv = aw // N_HEADS
    nh = ATTN_HEADS
    qb = kc = ATTN_KC
    assert seq % qb == 0 and qb % CHUNK == 0 and N_HEADS % nh == 0
    q3, k3 = (a.reshape(batch, seq, aw) for a in (q, k))
    qspec = pl.BlockSpec((None, qb, nh * hv), lambda b, h, i: (b, i, h))
    out = pl.pallas_call(
        functools.partial(_attn_prompt_kernel, nh=nh, lam_init=lam_init),
        grid=(batch, N_HEADS // nh, seq // qb),
        in_specs=[_resident(lam_vecs.shape, lambda b, h, i: (0, 0)),
                  qspec,
                  pl.BlockSpec((None, seq, nh * hv), lambda b, h, i: (b, 0, h)),
                  pl.BlockSpec((None, nh * hv, seq), lambda b, h, i: (b, h, 0)),
                  _resident((1, hv), lambda b, h, i: (0, 0))],
        out_specs=qspec,
        out_shape=jax.ShapeDtypeStruct((batch, seq, aw), BF16),
        scratch_shapes=[pltpu.VMEM((nh, hv, 2 * qb), BF16),
                        pltpu.VMEM((nh, kc, 2 * qb), F32), pltpu.VMEM((nh, kc, 2 * qb), F32),
                        pltpu.VMEM((nh, 1, 2 * qb), F32), pltpu.VMEM((nh, 1, 2 * qb), F32),
                        pltpu.VMEM((nh, hv, 2 * qb), F32)],
        compiler_params=_params("parallel", "parallel", "arbitrary"),
        name="attn_prompt",
    )(lam_vecs, q3, k3, vt3, subln_g.reshape(1, hv))
    return out.reshape(batch * seq, aw)


def _attn_sample_kernel(lam_ref, q_ref, kc_ref, vc_ref, kn_ref, vn_ref, g_ref, o_ref, *, lam_init):
    t = q_ref.shape[0]
    hv = q_ref.shape[-1] // N_HEADS
    past = kc_ref.shape[0] // N_HEADS
    lam = _lam(lam_ref, lam_init)
    for h in range(N_HEADS):
        cols = slice(h * hv, (h + 1) * hv)
        qq = _split_halves(q_ref[:, cols])
        kc = kc_ref[pl.ds(h, past, stride=N_HEADS), :].astype(BF16)
        vc = vc_ref[pl.ds(h, past, stride=N_HEADS), :].astype(BF16)
        s_c = _dot_nt(qq, kc)
        s_n = _dot_nt(qq, kn_ref[:, cols])
        m = jnp.maximum(jnp.max(s_c, axis=-1, keepdims=True), jnp.max(s_n, axis=-1, keepdims=True))
        e_c = jnp.exp2(s_c - m)
        e_n = jnp.exp2(s_n - m)
        l = jnp.sum(e_c, axis=-1, keepdims=True) + jnp.sum(e_n, axis=-1, keepdims=True)
        acc = _dot(e_c.astype(BF16), vc) + _dot(e_n.astype(BF16), vn_ref[:, cols])
        oo = acc * (1.0 / l)
        o = oo[:t] - lam * oo[t:]
        o_ref[:, cols] = (_rms(o, g_ref[...]) * (1.0 - lam_init)).astype(BF16)


def _attn_sample(q, k, v, cache_k4, cache_v4, layer, lam_vecs, subln_g, lam_init, batch, seq):
    aw = q.shape[-1]
    hv = aw // N_HEADS
    q3, k3, v3 = (a.reshape(batch, seq, aw) for a in (q, k, v))
    new = pl.BlockSpec((None, seq, aw), lambda b: (b, 0, 0))
    cache = pl.BlockSpec((None, None, cache_k4.shape[2], hv), lambda b: (layer, b, 0, 0))
    out = pl.pallas_call(
        functools.partial(_attn_sample_kernel, lam_init=lam_init),
        grid=(batch,),
        in_specs=[_resident(lam_vecs.shape, lambda b: (0, 0)),
                  new, cache, cache, new, new,
                  _resident((1, hv), lambda b: (0, 0))],
        out_specs=new,
        out_shape=jax.ShapeDtypeStruct((batch, seq, aw), BF16),
        compiler_params=_params("parallel"),
        name="attn_sample",
    )(lam_vecs, q3, cache_k4, cache_v4, k3, v3, subln_g.reshape(1, hv))
    return out.reshape(batch * seq, aw)


CONV_ROWS = 128


def _conv_kernel(*refs, tt, has_hist, multi_tile):
    rows = CONV_ROWS if tt % CONV_ROWS == 0 else HALO
    refs = list(refs)
    u_ref = refs.pop(0)
    prev_ref = refs.pop(0) if multi_tile else None
    hist_ref = refs.pop(0) if has_hist else None
    w_ref, b_ref, lg_ref, lb_ref, c_ref, state_ref, ext = refs
    cw = u_ref.shape[-1]
    t = pl.program_id(1)

    def first_tile_halo():
        ext[0:HALO, :] = jnp.zeros((HALO, cw), F32)
        if has_hist:
            ext[HALO - CONV_BUF:HALO, :] = hist_ref[...]

    if multi_tile:
        pl.when(t == 0)(first_tile_halo)

        @pl.when(t > 0)
        def _():
            ext[0:HALO, :] = prev_ref[...]
    else:
        first_tile_halo()
    ext[HALO:HALO + tt, :] = u_ref[...]

    bias = b_ref[...]
    lg = lg_ref[...]
    lb = lb_ref[...]
    base = HALO - CONV_BUF

    def group(r, carry):
        r0 = pl.multiple_of(r * rows, rows)
        win = ext[pl.ds(r0, HALO + rows), :]
        acc = jnp.broadcast_to(bias, (rows, cw))
        nwin = HALO + rows
        for b in range(SUBLANES):
            shifted = pltpu.roll(win, (nwin - base - b) % nwin, axis=0)
            for j in range(b, DW_WIDTH, SUBLANES):
                acc = acc + w_ref[j:j + 1, :] * shifted[j - b:j - b + rows, :]
        mu = jnp.mean(acc, axis=-1, keepdims=True)
        xc = acc - mu
        var = jnp.mean(xc * xc, axis=-1, keepdims=True)
        y = xc * lax.rsqrt(var + EPS) * lg + lb
        c_ref[pl.ds(r0, rows), :] = (y * jax.nn.sigmoid(y)).astype(BF16)
        return carry

    lax.fori_loop(0, tt // rows, group, 0)

    @pl.when(t == pl.num_programs(1) - 1)
    def _():
        state_ref[...] = ext[HALO + tt - CONV_BUF:HALO + tt, :]


def _conv_branch(u, hist4, layer, dw_w, dw_b, ln_g, ln_b, batch, seq):
    cw = u.shape[-1]
    tt = _tile(seq, 512)
    nt = seq // tt
    multi_tile = nt > 1
    has_hist = hist4 is not None
    u3 = u.reshape(batch, seq, cw)
    args = [u3]
    in_specs = [pl.BlockSpec((None, tt, cw), lambda b, t: (b, t, 0))]
    if multi_tile:
        per = tt // HALO
        args.append(u3)
        in_specs.append(pl.BlockSpec((None, HALO, cw), lambda b, t: (b, jnp.maximum(t * per - 1, 0), 0)))
    if has_hist:
        args.append(hist4)
        in_specs.append(pl.BlockSpec((None, None, CONV_BUF, cw), lambda b, t: (layer, b, 0, 0)))
    vec = _resident((1, cw), lambda b, t: (0, 0))
    args += [dw_w, dw_b.reshape(1, cw), ln_g.reshape(1, cw), ln_b.reshape(1, cw)]
    in_specs += [_resident((DW_WIDTH, cw), lambda b, t: (0, 0)), vec, vec, vec]
    out, state = pl.pallas_call(
        functools.partial(_conv_kernel, tt=tt, has_hist=has_hist, multi_tile=multi_tile),
        grid=(batch, nt),
        in_specs=in_specs,
        out_specs=[pl.BlockSpec((None, tt, cw), lambda b, t: (b, t, 0)),
                   pl.BlockSpec((None, CONV_BUF, cw), lambda b, t: (b, 0, 0))],
        out_shape=[jax.ShapeDtypeStruct((batch, seq, cw), BF16),
                   jax.ShapeDtypeStruct((batch, CONV_BUF, cw), F32)],
        scratch_shapes=[pltpu.VMEM((HALO + tt, cw), F32)],
        compiler_params=_params("parallel", "arbitrary"),
        name="conv_branch",
    )(*args)
    return out.reshape(batch * seq, cw), state


def _out_proj_kernel(h_ref, o_ref, c_ref, w_ref, out_ref, *, aw):
    out_ref[...] = h_ref[...] + _dot(o_ref[...], w_ref[0:aw, :]) + _dot(c_ref[...], w_ref[aw:, :])


def _out_proj(h, o, c, w_o_b, layer):
    n, d = h.shape
    aw = o.shape[-1]
    tm = _tile(n, 512)
    row = lambda width: pl.BlockSpec((tm, width), lambda r: (r, 0))
    return pl.pallas_call(
        functools.partial(_out_proj_kernel, aw=aw),
        grid=(n // tm,),
        in_specs=[row(d), row(aw), row(c.shape[-1]), _resident((None, d, d), lambda r: (layer, 0, 0))],
        out_specs=row(d),
        out_shape=jax.ShapeDtypeStruct((n, d), F32),
        compiler_params=_params("parallel"),
        name="out_proj",
    )(h, o, c, w_o_b)


def _swiglu_chunk(xn, wg, wu, wd):
    g = _dot(xn, wg)
    u = _dot(xn, wu)
    return _dot((g * jax.nn.sigmoid(g) * u).astype(BF16), wd)


def _ffn_dense_kernel(h_ref, g_ref, wg_ref, wu_ref, wd_ref, out_ref, *, fc):
    x = h_ref[...]
    xn = _rms(x, g_ref[...]).astype(BF16)
    acc = x
    for lo in range(0, wg_ref.shape[-1], fc):
        acc = acc + _swiglu_chunk(xn, wg_ref[:, lo:lo + fc], wu_ref[:, lo:lo + fc], wd_ref[lo:lo + fc, :])
    out_ref[...] = acc


def _ffn_dense(h, g, wg_b, wu_b, wd_b, j):
    n, d = h.shape
    f = wg_b.shape[-1]
    tm = _tile(n, 512)
    fc = f // 2
    row = pl.BlockSpec((tm, d), lambda r: (r, 0))
    return pl.pallas_call(
        functools.partial(_ffn_dense_kernel, fc=fc),
        grid=(n // tm,),
        in_specs=[row, _resident((1, d), lambda r: (0, 0)),
                  _resident((None, d, f), lambda r: (j, 0, 0)),
                  _resident((None, d, f), lambda r: (j, 0, 0)),
                  _resident((None, f, d), lambda r: (j, 0, 0))],
        out_specs=row,
        out_shape=jax.ShapeDtypeStruct((n, d), F32),
        compiler_params=_params("parallel"),
        name="ffn_dense",
    )(h, g.reshape(1, d), wg_b, wu_b, wd_b)


ROUTER_LANES = 128
TOP_K = 2


def _to_token_tiles(ref, x):
    rows, d = x.shape
    per = d // LANES
    for s in range(per):
        ref[pl.ds(s, rows, stride=per), :] = x[:, s * LANES:(s + 1) * LANES]


def _from_token_tiles(ref, rows):
    per = ref.shape[0] // rows
    return jnp.concatenate([ref[pl.ds(s, rows, stride=per), :] for s in range(per)], axis=-1)


def _router_kernel(h_ref, g_ref, rw_ref, idx_ref, gate_ref, xt_ref, *, n_experts):
    xn = _rms(h_ref[...], g_ref[...])
    _to_token_tiles(xt_ref, xn)
    logits = _dot(xn.astype(BF16), rw_ref[...])
    lane = lax.broadcasted_iota(jnp.int32, logits.shape, 1)
    neg = jnp.float32(-jnp.inf)
    lg = jnp.where(lane < n_experts, logits, neg)
    m1 = jnp.max(lg, axis=-1, keepdims=True)
    i1 = jnp.min(jnp.where(lg == m1, lane, ROUTER_LANES), axis=-1, keepdims=True)
    lg2 = jnp.where(lane == i1, neg, lg)
    m2 = jnp.max(lg2, axis=-1, keepdims=True)
    i2 = jnp.min(jnp.where(lg2 == m2, lane, ROUTER_LANES), axis=-1, keepdims=True)
    e2 = jnp.exp(m2 - m1)
    den = 1.0 + e2
    idx_ref[...] = jnp.where(lane == 0, i1, jnp.where(lane == 1, i2, 0))
    gate_ref[...] = jnp.where(lane == 0, 1.0 / den, jnp.where(lane == 1, e2 / den, 0.0))


def _router(h, g, router_b, j, n_experts):
    n, d = h.shape
    tm = _tile(n, 512)
    lanes = pl.BlockSpec((tm, ROUTER_LANES), lambda r: (r, 0))
    per = d // LANES
    return pl.pallas_call(
        functools.partial(_router_kernel, n_experts=n_experts),
        grid=(n // tm,),
        in_specs=[pl.BlockSpec((tm, d), lambda r: (r, 0)), _resident((1, d), lambda r: (0, 0)),
                  _resident((None, d, ROUTER_LANES), lambda r: (j, 0, 0))],
        out_specs=[lanes, lanes, pl.BlockSpec((tm * per, LANES), lambda r: (r, 0))],
        out_shape=[jax.ShapeDtypeStruct((n, ROUTER_LANES), jnp.int32),
                   jax.ShapeDtypeStruct((n, ROUTER_LANES), F32),
                   jax.ShapeDtypeStruct((n * per, LANES), F32)],
        compiler_params=_params("parallel"),
        name="router",
    )(h, g.reshape(1, d), router_b)


def _dispatch_plan(top_i, n_experts, tm):
    n = top_i.shape[0]
    n_slots = TOP_K * n
    n_tiles = -(-n_slots // tm) + n_experts
    n_pos = n_tiles * tm
    e_flat = top_i.reshape(-1)
    order = jnp.argsort(e_flat, stable=True).astype(jnp.int32)
    counts = jnp.sum((e_flat[:, None] == jnp.arange(n_experts, dtype=jnp.int32)[None, :]).astype(jnp.int32), axis=0)
    padded = -(-counts // tm) * tm
    ends = jnp.cumsum(padded)
    off = ends - padded
    coff = jnp.cumsum(counts) - counts
    pos = jnp.arange(n_pos, dtype=jnp.int32)
    e_of = jnp.minimum(jnp.searchsorted(ends, pos, side="right").astype(jnp.int32), n_experts - 1)
    r = pos - off[e_of]
    valid = (pos < ends[-1]) & (r < counts[e_of])
    slot = order[jnp.clip(coff[e_of] + r, 0, n_slots - 1)]
    tok = slot // TOP_K
    pad_rank = jnp.cumsum((~valid).astype(jnp.int32)) - 1
    src_tok = jnp.where(valid, tok, 0)
    dst_row = jnp.where(valid, (slot % TOP_K) * n + tok, n_slots + pad_rank)
    tile_pos = jnp.arange(n_tiles, dtype=jnp.int32) * tm
    tile_expert = e_of[tile_pos]
    tile_active = (tile_pos < ends[-1]).astype(jnp.int32)
    return src_tok.reshape(n_tiles, 1, tm), dst_row.reshape(n_tiles, 1, tm), tile_expert, tile_active


def _moe_kernel(te_ref, act_ref, src_ref, nsrc_ref, dst_ref, xt_hbm, wg_ref, wu_ref, wd_ref, y_hbm,
                xbuf, xn_scr, acc, ybuf, gsem, ssem, *, tm):
    t = pl.program_id(0)
    c = pl.program_id(1)
    n_t = pl.num_programs(0)
    n_c = pl.num_programs(1)
    per = xbuf.shape[0] // tm
    active = act_ref[t] == 1
    prev_active = (t > 0) & (act_ref[jnp.maximum(t - 1, 0)] == 1)
    next_active = (t + 1 < n_t) & (act_ref[jnp.minimum(t + 1, n_t - 1)] == 1)

    def tile_at(ref, row):
        return ref.at[pl.ds(pl.multiple_of(row * per, per), per)]

    def gather(idx_ref):
        def issue(r, carry):
            pltpu.make_async_copy(tile_at(xt_hbm, idx_ref[0, r]), tile_at(xbuf, r), gsem).start()
            return carry
        lax.fori_loop(0, tm, issue, 0, unroll=8)

    def wait_gather():
        def one(r, carry):
            pltpu.make_async_copy(tile_at(xt_hbm, 0), tile_at(xbuf, r), gsem).wait()
            return carry
        lax.fori_loop(0, tm, one, 0, unroll=8)

    def scatter():
        def issue(r, carry):
            pltpu.make_async_copy(tile_at(ybuf, r), tile_at(y_hbm, dst_ref[0, r]), ssem).start()
            return carry
        lax.fori_loop(0, tm, issue, 0, unroll=8)

    def wait_scatter():
        def one(r, carry):
            pltpu.make_async_copy(tile_at(ybuf, r), tile_at(y_hbm, 0), ssem).wait()
            return carry
        lax.fori_loop(0, tm, one, 0, unroll=8)

    @pl.when((c == 0) & (t == 0) & active)
    def _():
        gather(src_ref)

    @pl.when((c == 0) & active)
    def _():
        wait_gather()
        xn_scr[...] = _from_token_tiles(xbuf, tm).astype(BF16)

        pl.when(next_active)(lambda: gather(nsrc_ref))

    @pl.when(active)
    def _():
        y = _swiglu_chunk(xn_scr[...], wg_ref[...], wu_ref[...], wd_ref[...])

        @pl.when(c == 0)
        def _():
            acc[...] = y

        @pl.when((c > 0) & (c < n_c - 1))
        def _():
            acc[...] += y

        @pl.when(c == n_c - 1)
        def _():
            pl.when(prev_active)(wait_scatter)
            _to_token_tiles(ybuf, acc[...] + y)
            scatter()
            pl.when(t == n_t - 1)(wait_scatter)

    @pl.when((c == n_c - 1) & jnp.logical_not(active) & prev_active)
    def _():
        wait_scatter()


def _ffn_moe(h, g, router_b, wg_b, wu_b, wd_b, j):
    n, d = h.shape
    n_experts, f = wg_b.shape[1], wg_b.shape[-1]
    idx, gates, xt = _router(h, g, router_b, j, n_experts)
    tm = 1024 if n >= 8192 else 256
    src_tok, dst_row, tile_expert, tile_active = _dispatch_plan(idx[:, :TOP_K], n_experts, tm)
    n_tiles = src_tok.shape[0]
    per = d // LANES
    fc = 2 * V7X_MXU_DIM
    n_c = f // fc
    assert f % fc == 0 and n_c >= 2

    def wspec(shape, pick):
        return pl.BlockSpec(shape, lambda t, c, te, act: pick(te[t], jnp.where(act[t] == 1, c, n_c - 1)))

    smem_rows = lambda index_map: pl.BlockSpec((None, 1, tm), index_map, memory_space=pltpu.SMEM)
    grid_spec = pltpu.PrefetchScalarGridSpec(
        num_scalar_prefetch=2,
        grid=(n_tiles, n_c),
        in_specs=[smem_rows(lambda t, c, te, act: (t, 0, 0)),
                  smem_rows(lambda t, c, te, act: (jnp.minimum(t + 1, n_tiles - 1), 0, 0)),
                  smem_rows(lambda t, c, te, act: (t, 0, 0)),
                  pl.BlockSpec(memory_space=pl.ANY),
                  wspec((None, None, d, fc), lambda e, c: (j, e, 0, c)),
                  wspec((None, None, d, fc), lambda e, c: (j, e, 0, c)),
                  wspec((None, None, fc, d), lambda e, c: (j, e, c, 0))],
        out_specs=pl.BlockSpec(memory_space=pl.ANY),
        scratch_shapes=[pltpu.VMEM((tm * per, LANES), F32), pltpu.VMEM((tm, d), BF16), pltpu.VMEM((tm, d), F32),
                        pltpu.VMEM((tm * per, LANES), F32),
                        pltpu.SemaphoreType.DMA, pltpu.SemaphoreType.DMA],
    )
    y = pl.pallas_call(
        functools.partial(_moe_kernel, tm=tm),
        grid_spec=grid_spec,
        out_shape=jax.ShapeDtypeStruct((n_tiles * tm * per, LANES), F32),
        compiler_params=_params("arbitrary", "arbitrary"),
        name="ffn_moe",
    )(tile_expert, tile_active, src_tok, src_tok, dst_row, xt, wg_b, wu_b, wd_b)
    return y, gates


def _ple_kernel(*refs, final, routed):
    refs = list(refs)
    h_ref = refs.pop(0)
    x = h_ref[...]
    if routed:
        y0_ref, y1_ref, gate_ref = refs.pop(0), refs.pop(0), refs.pop(0)
        gates = gate_ref[...]
        rows = x.shape[0]
        x = x + gates[:, 0:1] * _from_token_tiles(y0_ref, rows) + gates[:, 1:2] * _from_token_tiles(y1_ref, rows)
    p_ref, g_ref, wg_ref, wp_ref = refs[:4]
    out_ref = refs[-1]
    gate = jax.nn.sigmoid(_dot(_rms(x, g_ref[...]).astype(BF16), wg_ref[...]))
    h3 = x + gate * _dot(p_ref[...].astype(BF16), wp_ref[...])
    out_ref[...] = _rms(h3, refs[4][...]) if final else h3


def _ple(h, routed, p3, layer, g, gate_w_b, proj_w_b, final_g):
    n, d = h.shape
    dp = p3.shape[-1]
    tm = _tile(n, 512)
    final = final_g is not None
    row = pl.BlockSpec((tm, d), lambda r: (r, 0))
    vec = _resident((1, d), lambda r: (0, 0))
    args, in_specs = [h], [row]
    if routed is not None:
        y, gates = routed
        steps = n // tm
        tiles = pl.BlockSpec((tm * (d // LANES), LANES), lambda r: (r, 0))
        second = pl.BlockSpec((tm * (d // LANES), LANES), lambda r: (steps + r, 0))
        args += [y, y, gates]
        in_specs += [tiles, second, pl.BlockSpec((tm, ROUTER_LANES), lambda r: (r, 0))]
    args += [p3, g.reshape(1, d), gate_w_b, proj_w_b]
    in_specs += [pl.BlockSpec((None, tm, dp), lambda r: (layer, r, 0)), vec,
                 _resident((None, d, d), lambda r: (layer, 0, 0)),
                 _resident((None, dp, d), lambda r: (layer, 0, 0))]
    if final:
        args.append(final_g.reshape(1, d))
        in_specs.append(vec)
    return pl.pallas_call(
        functools.partial(_ple_kernel, final=final, routed=routed is not None),
        grid=(n // tm,),
        in_specs=in_specs,
        out_specs=row,
        out_shape=jax.ShapeDtypeStruct((n, d), F32),
        compiler_params=_params("parallel"),
        name="ple",
    )(*args)


def _trunk(x, p, cache_k, cache_v, state_conv, wts):
    batch, seq, d = x.shape
    depth = wts["w_in"].shape[0]
    aw = d // 2
    cw = d - aw
    n = batch * seq
    h = x.reshape(n, d)
    p3 = p.reshape(depth, n, p.shape[-1])
    prompt = cache_k is None
    if not prompt:
        cache_k4 = cache_k.reshape(cache_k.shape[:2] + (-1, cache_k.shape[-1]))
        cache_v4 = cache_v.reshape(cache_v.shape[:2] + (-1, cache_v.shape[-1]))
    kv_all = tuple(jnp.zeros((depth, n * N_HEADS, aw // N_HEADS), F32) for _ in range(2))
    cs = []
    for i in range(depth):
        lam_init = 0.8 - 0.6 * math.exp(-0.3 * i)
        lam_vecs = jnp.stack([wts["lam_q1"][i], wts["lam_k1"][i], wts["lam_q2"][i], wts["lam_k2"][i]])
        q, k_all, v_all, kb, vb, u = _in_proj(h, wts["norm_mix_g"][i], wts["w_in"], i, kv_all, aw, cw, seq,
                                              transpose_v=prompt)
        kv_all = (k_all, v_all)
        if prompt:
            o = _attn_prompt(q, kb, vb, lam_vecs, wts["subln_g"][i], lam_init, batch, seq)
            hist = None
        else:
            o = _attn_sample(q, kb, vb, cache_k4, cache_v4, i, lam_vecs, wts["subln_g"][i], lam_init, batch, seq)
            hist = state_conv
        c, conv_state = _conv_branch(u, hist, i, wts["dw_w"][i], wts["dw_b"][i], wts["conv_ln_g"][i],
                                     wts["conv_ln_b"][i], batch, seq)
        h = _out_proj(h, o, c, wts["w_o"], i)
        j = i // 2
        if i % 2 == 0:
            h = _ffn_dense(h, wts["norm_ffn_g"][i], wts["ff_wg"], wts["ff_wu"], wts["ff_wd"], j)
            routed = None
        else:
            routed = _ffn_moe(h, wts["norm_ffn_g"][i], wts["router_w"], wts["moe_wg"], wts["moe_wu"],
                              wts["moe_wd"], j)
        h = _ple(h, routed, p3, i, wts["ple_norm_g"][i], wts["ple_gate_w"], wts["ple_proj_w"],
                 wts["final_norm_g"] if i == depth - 1 else None)
        cs.append(conv_state)
    kv_shape = (depth, batch, seq, N_HEADS, aw // N_HEADS)
    return h.reshape(batch, seq, d), kv_all[0].reshape(kv_shape), kv_all[1].reshape(kv_shape), jnp.stack(cs)


def kernel(x_prompt, x_sample, cache_k, cache_v, state_conv, p_prompt, p_sample, norm_mix_g, w_in, lam_q1, lam_k1, lam_q2, lam_k2, subln_g, dw_w, dw_b, conv_ln_g, conv_ln_b, w_o, norm_ffn_g, ff_wg, ff_wu, ff_wd, router_w, moe_wg, moe_wu, moe_wd, ple_norm_g, ple_gate_w, ple_proj_w, final_norm_g):
    n_experts = router_w.shape[-1]
    wts = dict(
        norm_mix_g=norm_mix_g, lam_q1=lam_q1, lam_k1=lam_k1, lam_q2=lam_q2, lam_k2=lam_k2, subln_g=subln_g,
        dw_w=dw_w, dw_b=dw_b, conv_ln_g=conv_ln_g, conv_ln_b=conv_ln_b, norm_ffn_g=norm_ffn_g,
        ple_norm_g=ple_norm_g, final_norm_g=final_norm_g,
        w_in=w_in.astype(BF16), w_o=w_o.astype(BF16),
        ff_wg=ff_wg.astype(BF16), ff_wu=ff_wu.astype(BF16), ff_wd=ff_wd.astype(BF16),
        router_w=jnp.pad(router_w, ((0, 0), (0, 0), (0, ROUTER_LANES - n_experts))).astype(BF16),
        moe_wg=moe_wg.astype(BF16), moe_wu=moe_wu.astype(BF16), moe_wd=moe_wd.astype(BF16),
        ple_gate_w=ple_gate_w.astype(BF16), ple_proj_w=ple_proj_w.astype(BF16),
    )
    y_p, k_p, v_p, c_p = _trunk(x_prompt, p_prompt, None, None, None, wts)
    y_s, k_s, v_s, c_s = _trunk(x_sample, p_sample, cache_k, cache_v, state_conv, wts)
    return (y_p, y_s, k_p, v_p, c_p, k_s, v_s, c_s)
```

```python
import functools
import math

import jax
import jax.numpy as jnp
from jax import lax
from jax.experimental import pallas as pl
from jax.experimental.pallas import tpu as pltpu

F32 = jnp.float32
BF16 = jnp.bfloat16

EPS = 1e-6
LOG2E = math.log2(math.e)
CHUNK = 64
N_HEADS = 4
DW_WIDTH = 31
CONV_BUF = DW_WIDTH - 1
HALO = 32
SUBLANES = 8
LANES = 128
V7X_MXU_DIM = 256
V7X_VMEM_LIMIT = 56 * 1024 * 1024


def _params(*sem):
    return pltpu.CompilerParams(dimension_semantics=sem, vmem_limit_bytes=V7X_VMEM_LIMIT)


def _tile(n, pref):
    if n <= pref:
        return n
    t = pref
    while n % t or t % 8:
        t -= 1
    return t


def _rms(x, g):
    return x * lax.rsqrt(jnp.mean(x * x, axis=-1, keepdims=True) + EPS) * g


def _dot(a, b):
    return jnp.dot(a, b, preferred_element_type=F32)


def _dot_nt(a, b):
    return lax.dot_general(a, b, (((1,), (1,)), ((), ())), preferred_element_type=F32)


def _resident(shape, index_map):
    return pl.BlockSpec(shape, index_map, pipeline_mode=pl.Buffered(1))


def _in_proj_kernel(*refs, aw, cw, q_scale, transpose_v):
    h_ref, g_ref, w_ref = refs[:3]
    q_ref, kf_ref, vf_ref, kb_ref, vb_ref, u_ref = refs[-6:]
    xn = _rms(h_ref[...], g_ref[...]).astype(BF16)

    def mm(lo, width):
        return _dot(xn, w_ref[:, lo:lo + width])

    q_ref[...] = (mm(0, aw) * q_scale).astype(BF16)
    k = mm(aw, aw)
    _to_token_tiles(kf_ref, k)
    kb_ref[...] = k.astype(BF16)
    v = mm(2 * aw, aw)
    _to_token_tiles(vf_ref, v)
    vb_ref[...] = (v.T if transpose_v else v).astype(BF16)
    ga = mm(3 * aw, cw)
    gb = mm(3 * aw + cw, cw)
    u_ref[...] = ga * jax.nn.sigmoid(gb)


def _in_proj(h, g, w_in_b, layer, kv_all, aw, cw, seq, transpose_v):
    n, d = h.shape
    depth = w_in_b.shape[0]
    tm = _tile(seq, 512)
    per = seq // tm
    hv = aw // N_HEADS
    q_scale = (hv // 2) ** -0.5 * LOG2E
    row = lambda width: pl.BlockSpec((tm, width), lambda r: (r, 0))
    if transpose_v:
        vb_spec = pl.BlockSpec((None, aw, tm), lambda r: (r // per, 0, r % per))
        vb_shape = jax.ShapeDtypeStruct((n // seq, aw, seq), BF16)
    else:
        vb_spec, vb_shape = row(aw), jax.ShapeDtypeStruct((n, aw), BF16)
    all_spec = pl.BlockSpec((None, tm * N_HEADS, hv), lambda r: (layer, r, 0))
    all_shape = jax.ShapeDtypeStruct((depth, n * N_HEADS, hv), F32)
    args = [h, g.reshape(1, d), w_in_b, *kv_all]
    in_specs = [row(d),
                _resident((1, d), lambda r: (0, 0)),
                _resident((None, d, w_in_b.shape[-1]), lambda r: (layer, 0, 0)),
                pl.BlockSpec(memory_space=pl.ANY), pl.BlockSpec(memory_space=pl.ANY)]
    aliases = {3: 1, 4: 2}
    return pl.pallas_call(
        functools.partial(_in_proj_kernel, aw=aw, cw=cw, q_scale=q_scale, transpose_v=transpose_v),
        grid=(n // tm,),
        in_specs=in_specs,
        out_specs=[row(aw), all_spec, all_spec, row(aw), vb_spec, row(cw)],
        out_shape=[jax.ShapeDtypeStruct((n, aw), BF16), all_shape, all_shape,
                   jax.ShapeDtypeStruct((n, aw), BF16), vb_shape,
                   jax.ShapeDtypeStruct((n, cw), F32)],
        input_output_aliases=aliases,
        compiler_params=_params("parallel"),
        name="in_proj",
    )(*args)


def _split_halves(q):
    dk = q.shape[-1] // 2
    lane = lax.broadcasted_iota(jnp.int32, q.shape, 1)
    zero = jnp.zeros_like(q)
    return jnp.concatenate([jnp.where(lane < dk, q, zero), jnp.where(lane >= dk, q, zero)], axis=0)


def _lam(lam_ref, lam_init):
    lv = lam_ref[...]
    a1 = jnp.sum(lv[0:1] * lv[1:2], axis=-1, keepdims=True)
    a2 = jnp.sum(lv[2:3] * lv[3:4], axis=-1, keepdims=True)
    return jnp.exp(a1) - jnp.exp(a2) + lam_init


ATTN_KC = V7X_MXU_DIM
ATTN_HEADS = 2
ATTN_UNROLL = 4


def _attn_prompt_kernel(lam_ref, q_ref, k_ref, vt_ref, g_ref, o_ref, qt_scr, sa_scr, sb_scr, m_scr, l_scr, acc_scr, *,
                        nh, lam_init):
    kc = ATTN_KC
    qb = kc
    qi = pl.program_id(2)
    hv = q_ref.shape[-1] // nh
    heads = range(nh)
    for h in heads:
        qq = _split_halves(q_ref[:, h * hv:(h + 1) * hv].astype(F32))
        qt_scr[h] = qq.T.astype(BF16)
    m_scr[...] = jnp.full(m_scr.shape, -jnp.inf, F32)
    l_scr[...] = jnp.zeros(l_scr.shape, F32)
    acc_scr[...] = jnp.zeros(acc_scr.shape, F32)

    def scores(c, s_ref):
        off = pl.multiple_of(c * kc, kc)
        for h in heads:
            s_ref[h] = _dot(k_ref[pl.ds(off, kc), h * hv:(h + 1) * hv], qt_scr[h])

    def consume(c, s_ref, masked):
        off = pl.multiple_of(c * kc, kc)
        for h in heads:
            s = s_ref[h]
            if masked:
                key = lax.broadcasted_iota(jnp.int32, s.shape, 0)
                col = lax.broadcasted_iota(jnp.int32, s.shape, 1)
                s = jnp.where(key // CHUNK <= (col % qb) // CHUNK, s, -jnp.inf)
            m_prev = m_scr[h]
            m_new = jnp.maximum(m_prev, jnp.max(s, axis=0, keepdims=True))
            alpha = jnp.exp2(m_prev - m_new)
            p = jnp.exp2(s - m_new)
            l_scr[h] = alpha * l_scr[h] + jnp.sum(p, axis=0, keepdims=True)
            acc_scr[h] = alpha * acc_scr[h] + _dot(vt_ref[h * hv:(h + 1) * hv, pl.ds(off, kc)], p.astype(BF16))
            m_scr[h] = m_new

    scores(0, sa_scr)

    def pair(t, carry):
        c = 2 * t
        scores(c + 1, sb_scr)
        consume(c, sa_scr, False)
        scores(c + 2, sa_scr)
        consume(c + 1, sb_scr, False)
        return carry

    def unrolled(t, carry):
        for u in range(ATTN_UNROLL):
            pair(ATTN_UNROLL * t + u, carry)
        return carry

    n_long = qi // (2 * ATTN_UNROLL)
    lax.fori_loop(0, n_long, unrolled, 0)
    lax.fori_loop(ATTN_UNROLL * n_long, qi // 2, pair, 0)
    c0 = 2 * (qi // 2)

    @pl.when(qi % 2 == 1)
    def _():
        scores(c0 + 1, sb_scr)
        consume(c0, sa_scr, False)
        consume(c0 + 1, sb_scr, True)

    @pl.when(qi % 2 == 0)
    def _():
        consume(c0, sa_scr, True)

    lam = _lam(lam_ref, lam_init)
    for h in heads:
        oo = acc_scr[h] * (1.0 / l_scr[h])
        o = (oo[:, :qb] - lam * oo[:, qb:]).T
        o_ref[:, h * hv:(h + 1) * hv] = (_rms(o, g_ref[...]) * (1.0 - lam_init)).astype(BF16)


def _attn_prompt(q, k, vt3, lam_vecs, subln_g, lam_init, batch, seq):
    aw = q.shape[-1]
    hv = aw // N_HEADS
    nh = ATTN_HEADS
    qb = kc = ATTN_KC
    assert seq % qb == 0 and qb % CHUNK == 0 and N_HEADS % nh == 0
    q3, k3 = (a.reshape(batch, seq, aw) for a in (q, k))
    qspec = pl.BlockSpec((None, qb, nh * hv), lambda b, h, i: (b, i, h))
    out = pl.pallas_call(
        functools.partial(_attn_prompt_kernel, nh=nh, lam_init=lam_init),
        grid=(batch, N_HEADS // nh, seq // qb),
        in_specs=[_resident(lam_vecs.shape, lambda b, h, i: (0, 0)),
                  qspec,
                  pl.BlockSpec((None, seq, nh * hv), lambda b, h, i: (b, 0, h)),
                  pl.BlockSpec((None, nh * hv, seq), lambda b, h, i: (b, h, 0)),
                  _resident((1, hv), lambda b, h, i: (0, 0))],
        out_specs=qspec,
        out_shape=jax.ShapeDtypeStruct((batch, seq, aw), BF16),
        scratch_shapes=[pltpu.VMEM((nh, hv, 2 * qb), BF16),
                        pltpu.VMEM((nh, kc, 2 * qb), F32), pltpu.VMEM((nh, kc, 2 * qb), F32),
                        pltpu.VMEM((nh, 1, 2 * qb), F32), pltpu.VMEM((nh, 1, 2 * qb), F32),
                        pltpu.VMEM((nh, hv, 2 * qb), F32)],
        compiler_params=_params("parallel", "parallel", "arbitrary"),
        name="attn_prompt",
    )(lam_vecs, q3, k3, vt3, subln_g.reshape(1, hv))
    return out.reshape(batch * seq, aw)


def _attn_sample_kernel(lam_ref, q_ref, kc_ref, vc_ref, kn_ref, vn_ref, g_ref, o_ref, *, lam_init):
    t = q_ref.shape[0]
    hv = q_ref.shape[-1] // N_HEADS
    past = kc_ref.shape[0] // N_HEADS
    lam = _lam(lam_ref, lam_init)
    for h in range(N_HEADS):
        cols = slice(h * hv, (h + 1) * hv)
        qq = _split_halves(q_ref[:, cols])
        kc = kc_ref[pl.ds(h, past, stride=N_HEADS), :].astype(BF16)
        vc = vc_ref[pl.ds(h, past, stride=N_HEADS), :].astype(BF16)
        s_c = _dot_nt(qq, kc)
        s_n = _dot_nt(qq, kn_ref[:, cols])
        m = jnp.maximum(jnp.max(s_c, axis=-1, keepdims=True), jnp.max(s_n, axis=-1, keepdims=True))
        e_c = jnp.exp2(s_c - m)
        e_n = jnp.exp2(s_n - m)
        l = jnp.sum(e_c, axis=-1, keepdims=True) + jnp.sum(e_n, axis=-1, keepdims=True)
        acc = _dot(e_c.astype(BF16), vc) + _dot(e_n.astype(BF16), vn_ref[:, cols])
        oo = acc * (1.0 / l)
        o = oo[:t] - lam * oo[t:]
        o_ref[:, cols] = (_rms(o, g_ref[...]) * (1.0 - lam_init)).astype(BF16)


def _attn_sample(q, k, v, cache_k4, cache_v4, layer, lam_vecs, subln_g, lam_init, batch, seq):
    aw = q.shape[-1]
    hv = aw // N_HEADS
    q3, k3, v3 = (a.reshape(batch, seq, aw) for a in (q, k, v))
    new = pl.BlockSpec((None, seq, aw), lambda b: (b, 0, 0))
    cache = pl.BlockSpec((None, None, cache_k4.shape[2], hv), lambda b: (layer, b, 0, 0))
    out = pl.pallas_call(
        functools.partial(_attn_sample_kernel, lam_init=lam_init),
        grid=(batch,),
        in_specs=[_resident(lam_vecs.shape, lambda b: (0, 0)),
                  new, cache, cache, new, new,
                  _resident((1, hv), lambda b: (0, 0))],
        out_specs=new,
        out_shape=jax.ShapeDtypeStruct((batch, seq, aw), BF16),
        compiler_params=_params("parallel"),
        name="attn_sample",
    )(lam_vecs, q3, cache_k4, cache_v4, k3, v3, subln_g.reshape(1, hv))
    return out.reshape(batch * seq, aw)


CONV_ROWS = 128


def _conv_kernel(*refs, tt, has_hist, multi_tile):
    rows = CONV_ROWS if tt % CONV_ROWS == 0 else HALO
    refs = list(refs)
    u_ref = refs.pop(0)
    prev_ref = refs.pop(0) if multi_tile else None
    hist_ref = refs.pop(0) if has_hist else None
    w_ref, b_ref, lg_ref, lb_ref, c_ref, state_ref, ext = refs
    cw = u_ref.shape[-1]
    t = pl.program_id(1)

    def first_tile_halo():
        ext[0:HALO, :] = jnp.zeros((HALO, cw), F32)
        if has_hist:
            ext[HALO - CONV_BUF:HALO, :] = hist_ref[...]

    if multi_tile:
        pl.when(t == 0)(first_tile_halo)

        @pl.when(t > 0)
        def _():
            ext[0:HALO, :] = prev_ref[...]
    else:
        first_tile_halo()
    ext[HALO:HALO + tt, :] = u_ref[...]

    bias = b_ref[...]
    lg = lg_ref[...]
    lb = lb_ref[...]
    base = HALO - CONV_BUF

    def group(r, carry):
        r0 = pl.multiple_of(r * rows, rows)
        win = ext[pl.ds(r0, HALO + rows), :]
        acc = jnp.broadcast_to(bias, (rows, cw))
        nwin = HALO + rows
        for b in range(SUBLANES):
            shifted = pltpu.roll(win, (nwin - base - b) % nwin, axis=0)
            for j in range(b, DW_WIDTH, SUBLANES):
                acc = acc + w_ref[j:j + 1, :] * shifted[j - b:j - b + rows, :]
        mu = jnp.mean(acc, axis=-1, keepdims=True)
        xc = acc - mu
        var = jnp.mean(xc * xc, axis=-1, keepdims=True)
        y = xc * lax.rsqrt(var + EPS) * lg + lb
        c_ref[pl.ds(r0, rows), :] = (y * jax.nn.sigmoid(y)).astype(BF16)
        return carry

    lax.fori_loop(0, tt // rows, group, 0)

    @pl.when(t == pl.num_programs(1) - 1)
    def _():
        state_ref[...] = ext[HALO + tt - CONV_BUF:HALO + tt, :]


def _conv_branch(u, hist4, layer, dw_w, dw_b, ln_g, ln_b, batch, seq):
    cw = u.shape[-1]
    tt = _tile(seq, 512)
    nt = seq // tt
    multi_tile = nt > 1
    has_hist = hist4 is not None
    u3 = u.reshape(batch, seq, cw)
    args = [u3]
    in_specs = [pl.BlockSpec((None, tt, cw), lambda b, t: (b, t, 0))]
    if multi_tile:
        per = tt // HALO
        args.append(u3)
        in_specs.append(pl.BlockSpec((None, HALO, cw), lambda b, t: (b, jnp.maximum(t * per - 1, 0), 0)))
    if has_hist:
        args.append(hist4)
        in_specs.append(pl.BlockSpec((None, None, CONV_BUF, cw), lambda b, t: (layer, b, 0, 0)))
    vec = _resident((1, cw), lambda b, t: (0, 0))
    args += [dw_w, dw_b.reshape(1, cw), ln_g.reshape(1, cw), ln_b.reshape(1, cw)]
    in_specs += [_resident((DW_WIDTH, cw), lambda b, t: (0, 0)), vec, vec, vec]
    out, state = pl.pallas_call(
        functools.partial(_conv_kernel, tt=tt, has_hist=has_hist, multi_tile=multi_tile),
        grid=(batch, nt),
        in_specs=in_specs,
        out_specs=[pl.BlockSpec((None, tt, cw), lambda b, t: (b, t, 0)),
                   pl.BlockSpec((None, CONV_BUF, cw), lambda b, t: (b, 0, 0))],
        out_shape=[jax.ShapeDtypeStruct((batch, seq, cw), BF16),
                   jax.ShapeDtypeStruct((batch, CONV_BUF, cw), F32)],
        scratch_shapes=[pltpu.VMEM((HALO + tt, cw), F32)],
        compiler_params=_params("parallel", "arbitrary"),
        name="conv_branch",
    )(*args)
    return out.reshape(batch * seq, cw), state


def _out_proj_kernel(h_ref, o_ref, c_ref, w_ref, out_ref, *, aw):
    out_ref[...] = h_ref[...] + _dot(o_ref[...], w_ref[0:aw, :]) + _dot(c_ref[...], w_ref[aw:, :])


def _out_proj(h, o, c, w_o_b, layer):
    n, d = h.shape
    aw = o.shape[-1]
    tm = _tile(n, 512)
    row = lambda width: pl.BlockSpec((tm, width), lambda r: (r, 0))
    return pl.pallas_call(
        functools.partial(_out_proj_kernel, aw=aw),
        grid=(n // tm,),
        in_specs=[row(d), row(aw), row(c.shape[-1]), _resident((None, d, d), lambda r: (layer, 0, 0))],
        out_specs=row(d),
        out_shape=jax.ShapeDtypeStruct((n, d), F32),
        compiler_params=_params("parallel"),
        name="out_proj",
    )(h, o, c, w_o_b)


def _swiglu_chunk(xn, wg, wu, wd):
    g = _dot(xn, wg)
    u = _dot(xn, wu)
    return _dot((g * jax.nn.sigmoid(g) * u).astype(BF16), wd)


def _ffn_dense_kernel(h_ref, g_ref, wg_ref, wu_ref, wd_ref, out_ref, *, fc):
    x = h_ref[...]
    xn = _rms(x, g_ref[...]).astype(BF16)
    acc = x
    for lo in range(0, wg_ref.shape[-1], fc):
        acc = acc + _swiglu_chunk(xn, wg_ref[:, lo:lo + fc], wu_ref[:, lo:lo + fc], wd_ref[lo:lo + fc, :])
    out_ref[...] = acc


def _ffn_dense(h, g, wg_b, wu_b, wd_b, j):
    n, d = h.shape
    f = wg_b.shape[-1]
    tm = _tile(n, 512)
    fc = f // 2
    row = pl.BlockSpec((tm, d), lambda r: (r, 0))
    return pl.pallas_call(
        functools.partial(_ffn_dense_kernel, fc=fc),
        grid=(n // tm,),
        in_specs=[row, _resident((1, d), lambda r: (0, 0)),
                  _resident((None, d, f), lambda r: (j, 0, 0)),
                  _resident((None, d, f), lambda r: (j, 0, 0)),
                  _resident((None, f, d), lambda r: (j, 0, 0))],
        out_specs=row,
        out_shape=jax.ShapeDtypeStruct((n, d), F32),
        compiler_params=_params("parallel"),
        name="ffn_dense",
    )(h, g.reshape(1, d), wg_b, wu_b, wd_b)


ROUTER_LANES = 128
TOP_K = 2


def _to_token_tiles(ref, x):
    rows, d = x.shape
    per = d // LANES
    for s in range(per):
        ref[pl.ds(s, rows, stride=per), :] = x[:, s * LANES:(s + 1) * LANES]


def _from_token_tiles(ref, rows):
    per = ref.shape[0] // rows
    return jnp.concatenate([ref[pl.ds(s, rows, stride=per), :] for s in range(per)], axis=-1)


def _router_kernel(h_ref, g_ref, rw_ref, idx_ref, gate_ref, xt_ref, *, n_experts):
    xn = _rms(h_ref[...], g_ref[...])
    _to_token_tiles(xt_ref, xn)
    logits = _dot(xn.astype(BF16), rw_ref[...])
    lane = lax.broadcasted_iota(jnp.int32, logits.shape, 1)
    neg = jnp.float32(-jnp.inf)
    lg = jnp.where(lane < n_experts, logits, neg)
    m1 = jnp.max(lg, axis=-1, keepdims=True)
    i1 = jnp.min(jnp.where(lg == m1, lane, ROUTER_LANES), axis=-1, keepdims=True)
    lg2 = jnp.where(lane == i1, neg, lg)
    m2 = jnp.max(lg2, axis=-1, keepdims=True)
    i2 = jnp.min(jnp.where(lg2 == m2, lane, ROUTER_LANES), axis=-1, keepdims=True)
    e2 = jnp.exp(m2 - m1)
    den = 1.0 + e2
    idx_ref[...] = jnp.where(lane == 0, i1, jnp.where(lane == 1, i2, 0))
    gate_ref[...] = jnp.where(lane == 0, 1.0 / den, jnp.where(lane == 1, e2 / den, 0.0))


def _router(h, g, router_b, j, n_experts):
    n, d = h.shape
    tm = _tile(n, 512)
    lanes = pl.BlockSpec((tm, ROUTER_LANES), lambda r: (r, 0))
    per = d // LANES
    return pl.pallas_call(
        functools.partial(_router_kernel, n_experts=n_experts),
        grid=(n // tm,),
        in_specs=[pl.BlockSpec((tm, d), lambda r: (r, 0)), _resident((1, d), lambda r: (0, 0)),
                  _resident((None, d, ROUTER_LANES), lambda r: (j, 0, 0))],
        out_specs=[lanes, lanes, pl.BlockSpec((tm * per, LANES), lambda r: (r, 0))],
        out_shape=[jax.ShapeDtypeStruct((n, ROUTER_LANES), jnp.int32),
                   jax.ShapeDtypeStruct((n, ROUTER_LANES), F32),
                   jax.ShapeDtypeStruct((n * per, LANES), F32)],
        compiler_params=_params("parallel"),
        name="router",
    )(h, g.reshape(1, d), router_b)


def _dispatch_plan(top_i, n_experts, tm):
    n = top_i.shape[0]
    n_slots = TOP_K * n
    n_tiles = -(-n_slots // tm) + n_experts
    n_pos = n_tiles * tm
    e_flat = top_i.reshape(-1)
    order = jnp.argsort(e_flat, stable=True).astype(jnp.int32)
    counts = jnp.sum((e_flat[:, None] == jnp.arange(n_experts, dtype=jnp.int32)[None, :]).astype(jnp.int32), axis=0)
    padded = -(-counts // tm) * tm
    ends = jnp.cumsum(padded)
    off = ends - padded
    coff = jnp.cumsum(counts) - counts
    pos = jnp.arange(n_pos, dtype=jnp.int32)
    e_of = jnp.minimum(jnp.searchsorted(ends, pos, side="right").astype(jnp.int32), n_experts - 1)
    r = pos - off[e_of]
    valid = (pos < ends[-1]) & (r < counts[e_of])
    slot = order[jnp.clip(coff[e_of] + r, 0, n_slots - 1)]
    tok = slot // TOP_K
    pad_rank = jnp.cumsum((~valid).astype(jnp.int32)) - 1
    src_tok = jnp.where(valid, tok, 0)
    dst_row = jnp.where(valid, (slot % TOP_K) * n + tok, n_slots + pad_rank)
    tile_pos = jnp.arange(n_tiles, dtype=jnp.int32) * tm
    tile_expert = e_of[tile_pos]
    tile_active = (tile_pos < ends[-1]).astype(jnp.int32)
    return src_tok.reshape(n_tiles, 1, tm), dst_row.reshape(n_tiles, 1, tm), tile_expert, tile_active


def _moe_kernel(te_ref, act_ref, src_ref, nsrc_ref, dst_ref, xt_hbm, wg_ref, wu_ref, wd_ref, y_hbm,
                xbuf, xn_scr, acc, ybuf, gsem, ssem, *, tm, n_chunks, spread):
    t = pl.program_id(0)
    c = pl.program_id(1)
    n_t = pl.num_programs(0)
    n_c = pl.num_programs(1)
    per = xbuf.shape[0] // tm
    active = act_ref[t] == 1
    prev_active = (t > 0) & (act_ref[jnp.maximum(t - 1, 0)] == 1)
    next_active = (t + 1 < n_t) & (act_ref[jnp.minimum(t + 1, n_t - 1)] == 1)

    def tile_at(ref, row):
        return ref.at[pl.ds(pl.multiple_of(row * per, per), per)]

    def gather(idx_ref):
        def issue(r, carry):
            pltpu.make_async_copy(tile_at(xt_hbm, idx_ref[0, r]), tile_at(xbuf, r), gsem).start()
            return carry
        lax.fori_loop(0, tm, issue, 0, unroll=8)

    def wait_gather():
        def one(r, carry):
            pltpu.make_async_copy(tile_at(xt_hbm, 0), tile_at(xbuf, r), gsem).wait()
            return carry
        lax.fori_loop(0, tm, one, 0, unroll=8)

    def scatter():
        def issue(r, carry):
            pltpu.make_async_copy(tile_at(ybuf, r), tile_at(y_hbm, dst_ref[0, r]), ssem).start()
            return carry
        lax.fori_loop(0, tm, issue, 0, unroll=8)

    def wait_scatter():
        def one(r, carry):
            pltpu.make_async_copy(tile_at(ybuf, r), tile_at(y_hbm, 0), ssem).wait()
            return carry
        lax.fori_loop(0, tm, one, 0, unroll=8)

    @pl.when((c == 0) & (t == 0) & active)
    def _():
        gather(src_ref)

    @pl.when((c == 0) & active)
    def _():
        wait_gather()
        xn_scr[...] = _from_token_tiles(xbuf, tm).astype(BF16)

        if not spread:
            pl.when(next_active)(lambda: gather(nsrc_ref))

    @pl.when((c == 0) & jnp.logical_not(active) & prev_active & spread)
    def _():
        wait_gather()

    @pl.when(active)
    def _():
        if spread:
            share = tm // n_chunks
            for i in range(share):
                r = c * share + i
                pltpu.make_async_copy(tile_at(xt_hbm, nsrc_ref[0, r]), tile_at(xbuf, r), gsem).start()
        y = _swiglu_chunk(xn_scr[...], wg_ref[...], wu_ref[...], wd_ref[...])

        @pl.when(c == 0)
        def _():
            acc[...] = y

        @pl.when((c > 0) & (c < n_c - 1))
        def _():
            acc[...] += y

        @pl.when(c == n_c - 1)
        def _():
            pl.when(prev_active)(wait_scatter)
            _to_token_tiles(ybuf, acc[...] + y)
            scatter()
            pl.when(t == n_t - 1)(wait_scatter)

    @pl.when((c == n_c - 1) & jnp.logical_not(active) & prev_active)
    def _():
        wait_scatter()


def _ffn_moe(h, g, router_b, wg_b, wu_b, wd_b, j):
    n, d = h.shape
    n_experts, f = wg_b.shape[1], wg_b.shape[-1]
    idx, gates, xt = _router(h, g, router_b, j, n_experts)
    per = d // LANES
    fc = 2 * V7X_MXU_DIM
    n_c = f // fc
    assert f % fc == 0 and n_c >= 2
    tm = (1024 // (n_c * LANES)) * n_c * LANES if n >= 8192 and n_c * LANES <= 1024 else 256
    src_tok, dst_row, tile_expert, tile_active = _dispatch_plan(idx[:, :TOP_K], n_experts, tm)
    n_tiles = src_tok.shape[0]

    def wspec(shape, pick):
        return pl.BlockSpec(shape, lambda t, c, te, act: pick(te[t], jnp.where(act[t] == 1, c, n_c - 1)))

    smem_rows = lambda index_map: pl.BlockSpec((None, 1, tm), index_map, memory_space=pltpu.SMEM)
    grid_spec = pltpu.PrefetchScalarGridSpec(
        num_scalar_prefetch=2,
        grid=(n_tiles, n_c),
        in_specs=[smem_rows(lambda t, c, te, act: (t, 0, 0)),
                  smem_rows(lambda t, c, te, act: (jnp.minimum(t + 1, n_tiles - 1), 0, 0)),
                  smem_rows(lambda t, c, te, act: (t, 0, 0)),
                  pl.BlockSpec(memory_space=pl.ANY),
                  wspec((None, None, d, fc), lambda e, c: (j, e, 0, c)),
                  wspec((None, None, d, fc), lambda e, c: (j, e, 0, c)),
                  wspec((None, None, fc, d), lambda e, c: (j, e, c, 0))],
        out_specs=pl.BlockSpec(memory_space=pl.ANY),
        scratch_shapes=[pltpu.VMEM((tm * per, LANES), F32), pltpu.VMEM((tm, d), BF16), pltpu.VMEM((tm, d), F32),
                        pltpu.VMEM((tm * per, LANES), F32),
                        pltpu.SemaphoreType.DMA, pltpu.SemaphoreType.DMA],
    )
    y = pl.pallas_call(
        functools.partial(_moe_kernel, tm=tm, n_chunks=n_c, spread=tm % n_c == 0),
        grid_spec=grid_spec,
        out_shape=jax.ShapeDtypeStruct((n_tiles * tm * per, LANES), F32),
        compiler_params=_params("arbitrary", "arbitrary"),
        name="ffn_moe",
    )(tile_expert, tile_active, src_tok, src_tok, dst_row, xt, wg_b, wu_b, wd_b)
    return y, gates


def _ple_kernel(*refs, final, routed):
    refs = list(refs)
    h_ref = refs.pop(0)
    x = h_ref[...]
    if routed:
        y0_ref, y1_ref, gate_ref = refs.pop(0), refs.pop(0), refs.pop(0)
        gates = gate_ref[...]
        rows = x.shape[0]
        x = x + gates[:, 0:1] * _from_token_tiles(y0_ref, rows) + gates[:, 1:2] * _from_token_tiles(y1_ref, rows)
    p_ref, g_ref, wg_ref, wp_ref = refs[:4]
    out_ref = refs[-1]
    gate = jax.nn.sigmoid(_dot(_rms(x, g_ref[...]).astype(BF16), wg_ref[...]))
    h3 = x + gate * _dot(p_ref[...].astype(BF16), wp_ref[...])
    out_ref[...] = _rms(h3, refs[4][...]) if final else h3


def _ple(h, routed, p3, layer, g, gate_w_b, proj_w_b, final_g):
    n, d = h.shape
    dp = p3.shape[-1]
    tm = _tile(n, 512)
    final = final_g is not None
    row = pl.BlockSpec((tm, d), lambda r: (r, 0))
    vec = _resident((1, d), lambda r: (0, 0))
    args, in_specs = [h], [row]
    if routed is not None:
        y, gates = routed
        steps = n // tm
        tiles = pl.BlockSpec((tm * (d // LANES), LANES), lambda r: (r, 0))
        second = pl.BlockSpec((tm * (d // LANES), LANES), lambda r: (steps + r, 0))
        args += [y, y, gates]
        in_specs += [tiles, second, pl.BlockSpec((tm, ROUTER_LANES), lambda r: (r, 0))]
    args += [p3, g.reshape(1, d), gate_w_b, proj_w_b]
    in_specs += [pl.BlockSpec((None, tm, dp), lambda r: (layer, r, 0)), vec,
                 _resident((None, d, d), lambda r: (layer, 0, 0)),
                 _resident((None, dp, d), lambda r: (layer, 0, 0))]
    if final:
        args.append(final_g.reshape(1, d))
        in_specs.append(vec)
    return pl.pallas_call(
        functools.partial(_ple_kernel, final=final, routed=routed is not None),
        grid=(n // tm,),
        in_specs=in_specs,
        out_specs=row,
        out_shape=jax.ShapeDtypeStruct((n, d), F32),
        compiler_params=_params("parallel"),
        name="ple",
    )(*args)


def _trunk(x, p, cache_k, cache_v, state_conv, wts):
    batch, seq, d = x.shape
    depth = wts["w_in"].shape[0]
    aw = d // 2
    cw = d - aw
    n = batch * seq
    h = x.reshape(n, d)
    p3 = p.reshape(depth, n, p.shape[-1])
    prompt = cache_k is None
    if not prompt:
        cache_k4 = cache_k.reshape(cache_k.shape[:2] + (-1, cache_k.shape[-1]))
        cache_v4 = cache_v.reshape(cache_v.shape[:2] + (-1, cache_v.shape[-1]))
    kv_all = tuple(jnp.zeros((depth, n * N_HEADS, aw // N_HEADS), F32) for _ in range(2))
    cs = []
    for i in range(depth):
        lam_init = 0.8 - 0.6 * math.exp(-0.3 * i)
        lam_vecs = jnp.stack([wts["lam_q1"][i], wts["lam_k1"][i], wts["lam_q2"][i], wts["lam_k2"][i]])
        q, k_all, v_all, kb, vb, u = _in_proj(h, wts["norm_mix_g"][i], wts["w_in"], i, kv_all, aw, cw, seq,
                                              transpose_v=prompt)
        kv_all = (k_all, v_all)
        if prompt:
            o = _attn_prompt(q, kb, vb, lam_vecs, wts["subln_g"][i], lam_init, batch, seq)
            hist = None
        else:
            o = _attn_sample(q, kb, vb, cache_k4, cache_v4, i, lam_vecs, wts["subln_g"][i], lam_init, batch, seq)
            hist = state_conv
        c, conv_state = _conv_branch(u, hist, i, wts["dw_w"][i], wts["dw_b"][i], wts["conv_ln_g"][i],
                                     wts["conv_ln_b"][i], batch, seq)
        h = _out_proj(h, o, c, wts["w_o"], i)
        j = i // 2
        if i % 2 == 0:
            h = _ffn_dense(h, wts["norm_ffn_g"][i], wts["ff_wg"], wts["ff_wu"], wts["ff_wd"], j)
            routed = None
        else:
            routed = _ffn_moe(h, wts["norm_ffn_g"][i], wts["router_w"], wts["moe_wg"], wts["moe_wu"],
                              wts["moe_wd"], j)
        h = _ple(h, routed, p3, i, wts["ple_norm_g"][i], wts["ple_gate_w"], wts["ple_proj_w"],
                 wts["final_norm_g"] if i == depth - 1 else None)
        cs.append(conv_state)
    kv_shape = (depth, batch, seq, N_HEADS, aw // N_HEADS)
    return h.reshape(batch, seq, d), kv_all[0].reshape(kv_shape), kv_all[1].reshape(kv_shape), jnp.stack(cs)


def kernel(x_prompt, x_sample, cache_k, cache_v, state_conv, p_prompt, p_sample, norm_mix_g, w_in, lam_q1, lam_k1, lam_q2, lam_k2, subln_g, dw_w, dw_b, conv_ln_g, conv_ln_b, w_o, norm_ffn_g, ff_wg, ff_wu, ff_wd, router_w, moe_wg, moe_wu, moe_wd, ple_norm_g, ple_gate_w, ple_proj_w, final_norm_g):
    n_experts = router_w.shape[-1]
    wts = dict(
        norm_mix_g=norm_mix_g, lam_q1=lam_q1, lam_k1=lam_k1, lam_q2=lam_q2, lam_k2=lam_k2, subln_g=subln_g,
        dw_w=dw_w, dw_b=dw_b, conv_ln_g=conv_ln_g, conv_ln_b=conv_ln_b, norm_ffn_g=norm_ffn_g,
        ple_norm_g=ple_norm_g, final_norm_g=final_norm_g,
        w_in=w_in.astype(BF16), w_o=w_o.astype(BF16),
        ff_wg=ff_wg.astype(BF16), ff_wu=ff_wu.astype(BF16), ff_wd=ff_wd.astype(BF16),
        router_w=jnp.pad(router_w, ((0, 0), (0, 0), (0, ROUTER_LANES - n_experts))).astype(BF16),
        moe_wg=moe_wg.astype(BF16), moe_wu=moe_wu.astype(BF16), moe_wd=moe_wd.astype(BF16),
        ple_gate_w=ple_gate_w.astype(BF16), ple_proj_w=ple_proj_w.astype(BF16),
    )
    y_p, k_p, v_p, c_p = _trunk(x_prompt, p_prompt, None, None, None, wts)
    y_s, k_s, v_s, c_s = _trunk(x_sample, p_sample, cache_k, cache_v, state_conv, wts)
    return (y_p, y_s, k_p, v_p, c_p, k_s, v_s, c_s)
```
